```python
import math
import jax, jax.numpy as jnp
from jax import lax
import numpy as np

D_MODEL = 1024
BATCH = 16
SEQ = 256
DEPTH = 2
DEC_BATCH = 8
DEC_SEQ = 2048
PAST_LEN = 512

GRID_W = 64
N_HEADS = 8
HEAD_DIM = 64
V_DIM = 2 * HEAD_DIM
QK_WIDTH = N_HEADS * 2 * HEAD_DIM
ATTN_WIDTH = N_HEADS * V_DIM
POOL_WINDOWS = (2, 4, 8, 16)
POOL_GROUP = 128
POOL_WIDTH = POOL_GROUP * len(POOL_WINDOWS)
FOURIER_GROUPS = 4
FOURIER_GROUP = 128
FOURIER_WIDTH = FOURIER_GROUPS * FOURIER_GROUP
IN_WIDTH = 2 * QK_WIDTH + ATTN_WIDTH + POOL_WIDTH + FOURIER_WIDTH
D_FF = 2816
N_MOD = 9
ROPE_BASE = 10000.0
Q_BLOCK = 128
EPS = 1e-6

kernel_name = 'hybrid_diffattn_pool_fourier_prefix_step'


def rms_norm(x, g):
    xf = x.astype(jnp.float32)
    y = xf * lax.rsqrt(jnp.mean(xf * xf, axis=-1, keepdims=True) + EPS)
    return (y * g.astype(jnp.float32)).astype(x.dtype)


def modulate(x, g, shift, scale):
    return rms_norm(x, g) * (1 + scale) + shift


def swiglu(h, w13, w2):
    gate, up = jnp.split(h @ w13, 2, axis=-1)
    return (jax.nn.silu(gate) * up) @ w2


def axial_rope_tables(L):
    rows = L // GRID_W
    row = jnp.repeat(jnp.arange(rows), GRID_W).astype(jnp.float32)
    col = jnp.tile(jnp.arange(GRID_W), rows).astype(jnp.float32)
    n_freq = HEAD_DIM // 4
    inv = ROPE_BASE ** (-jnp.arange(n_freq, dtype=jnp.float32) / n_freq)
    ang = jnp.stack([row[:, None] * inv, col[:, None] * inv], axis=1)
    return jnp.cos(ang), jnp.sin(ang)


def apply_axial_rope(x, cos, sin):
    B, L = x.shape[:2]
    xr = x.reshape(B, L, N_HEADS, 2, 2, 2, HEAD_DIM // 4)
    x1, x2 = xr[..., 0, :], xr[..., 1, :]
    c = cos[None, :, None, None]
    s = sin[None, :, None, None]
    out = jnp.stack([x1 * c - x2 * s, x1 * s + x2 * c], axis=-2)
    return out.reshape(x.shape).astype(x.dtype)


def diff_attention(q, k, v, lam):
    B, Lq = q.shape[:2]
    nb = Lq // Q_BLOCK
    qb = jnp.moveaxis(q.reshape(B, nb, Q_BLOCK, N_HEADS, 2, HEAD_DIM), 1, 0)
    kf = k.astype(jnp.float32)
    vf = v.astype(jnp.float32)
    lam_f = lam.astype(jnp.float32)
    scale = HEAD_DIM ** -0.5

    def block(qblk):
        s = jnp.einsum('bqhnd,bkhnd->bnhqk', qblk.astype(jnp.float32), kf) * scale
        p = jax.nn.softmax(s, axis=-1)
        a = p[:, 0] - lam_f * p[:, 1]
        return jnp.einsum('bhqk,bkhe->bqhe', a, vf)

    o = lax.map(block, qb)
    return jnp.moveaxis(o, 0, 1).reshape(B, Lq, N_HEADS, V_DIM).astype(v.dtype)


def pool_mix(u, w_pool, pool_scale):
    B, L, _ = u.shape
    G = len(POOL_WINDOWS)
    uf = u.astype(jnp.float32).reshape(B, L, G, POOL_GROUP)
    cs = jnp.concatenate([jnp.zeros((B, 1, G, POOL_GROUP), jnp.float32), jnp.cumsum(uf, axis=1)], axis=1)
    t = jnp.arange(L)
    outs = []
    for g, w in enumerate(POOL_WINDOWS):
        lo = jnp.clip(t - w // 2, 0, L)
        hi = jnp.clip(t + w // 2, 0, L)
        csg = cs[:, :, g]
        win_sum = jnp.take(csg, hi, axis=1) - jnp.take(csg, lo, axis=1)
        cnt = (hi - lo).astype(jnp.float32)[None, :, None]
        d = win_sum / cnt - uf[:, :, g]
        outs.append(jnp.einsum('blc,cd->bld', d, w_pool[g].astype(jnp.float32)))
    y = jnp.concatenate(outs, axis=-1) * pool_scale.astype(jnp.float32)
    return y.astype(u.dtype)


def fourier_mix(u):
    B, L, _ = u.shape
    uf = u.astype(jnp.float32).reshape(B, L, FOURIER_GROUPS, FOURIER_GROUP)
    y = jnp.fft.fft2(uf, axes=(1, 3), norm='ortho').real
    return y.reshape(B, L, FOURIER_WIDTH).astype(u.dtype)


def token_mixer(h, ctx_k, ctx_v, lam_init, w_in, q_norm_g, k_norm_g, lam_qk, subln_g,
                w_pool, pool_scale, w_gate, w_pa, w_pp, w_pf, w_out):
    B, L, _ = h.shape
    proj = h @ w_in
    q, k, v, up, uf = jnp.split(
        proj, [QK_WIDTH, 2 * QK_WIDTH, 2 * QK_WIDTH + ATTN_WIDTH,
               2 * QK_WIDTH + ATTN_WIDTH + POOL_WIDTH], axis=-1)
    q = rms_norm(q.reshape(B, L, N_HEADS, 2, HEAD_DIM), q_norm_g)
    k = rms_norm(k.reshape(B, L, N_HEADS, 2, HEAD_DIM), k_norm_g)
    v = v.reshape(B, L, N_HEADS, V_DIM)
    if ctx_k is None:
        k_all, v_all = k, v
        k_out, v_out = k.reshape(B, L, N_HEADS, 2 * HEAD_DIM), v
    else:
        cos, sin = axial_rope_tables(L)
        q = apply_axial_rope(q, cos, sin)
        k = apply_axial_rope(k, cos, sin)
        P = ctx_k.shape[1]
        k_all = jnp.concatenate([ctx_k.reshape(B, P, N_HEADS, 2, HEAD_DIM).astype(k.dtype), k], axis=1)
        v_all = jnp.concatenate([ctx_v.astype(v.dtype), v], axis=1)
        k_out, v_out = None, None
    lq = lam_qk.astype(jnp.float32)
    lam = jnp.exp(jnp.sum(lq[0] * lq[1])) - jnp.exp(jnp.sum(lq[2] * lq[3])) + lam_init
    o = diff_attention(q, k_all, v_all, lam)
    a = (rms_norm(o, subln_g) * (1 - lam_init)).reshape(B, L, ATTN_WIDTH)
    p = pool_mix(up, w_pool, pool_scale)
    f = fourier_mix(uf)
    ga, gp, gf = jnp.split(jax.nn.sigmoid(h @ w_gate), 3, axis=-1)
    m = ga * (a @ w_pa) + gp * (p @ w_pp) + gf * (f @ w_pf)
    return m @ w_out, k_out, v_out


def trunk_layer(x, cond, ctx_k, ctx_v, lam_init, w_ada, b_ada, norm_g, ffn_w13, ffn_w2, w_in,
                q_norm_g, k_norm_g, lam_qk, subln_g, w_pool, pool_scale, w_gate, w_pa, w_pp, w_pf, w_out):
    mod = (jax.nn.silu(cond) @ w_ada + b_ada).reshape(cond.shape[0], 1, N_MOD, D_MODEL)
    h = modulate(x, norm_g[0], mod[:, :, 0], mod[:, :, 1])
    x = x + 0.5 * mod[:, :, 2] * swiglu(h, ffn_w13[0], ffn_w2[0])
    h = modulate(x, norm_g[1], mod[:, :, 3], mod[:, :, 4])
    m, k_out, v_out = token_mixer(h, ctx_k, ctx_v, lam_init, w_in, q_norm_g, k_norm_g, lam_qk, subln_g,
                                  w_pool, pool_scale, w_gate, w_pa, w_pp, w_pf, w_out)
    x = x + mod[:, :, 5] * m
    h = modulate(x, norm_g[2], mod[:, :, 6], mod[:, :, 7])
    x = x + 0.5 * mod[:, :, 8] * swiglu(h, ffn_w13[1], ffn_w2[1])
    return x, k_out, v_out


def setup_inputs(seed: int = 0) -> dict:
    key = jax.random.key(seed)
    ks = jax.random.split(key, 24)
    D = D_MODEL

    def nrm(k, shape, scale):
        return jax.random.normal(k, shape, jnp.float32) * scale

    return {
        'x_prompt': nrm(ks[0], (BATCH, SEQ, D), 1.0),
        'x_sample': nrm(ks[1], (DEC_BATCH, DEC_SEQ, D), 1.0),
        'cache_k': nrm(ks[2], (DEC_BATCH, DEPTH, PAST_LEN, N_HEADS, 2 * HEAD_DIM), 1.0),
        'cache_v': nrm(ks[3], (DEC_BATCH, DEPTH, PAST_LEN, N_HEADS, V_DIM), 1.0),
        'c': nrm(ks[4], (DEC_BATCH, D), 1.0),
        'c_ctx': nrm(ks[5], (D,), 1.0),
        'w_ada': nrm(ks[6], (DEPTH, D, N_MOD * D), 0.5 * D ** -0.5),
        'b_ada': nrm(ks[7], (DEPTH, N_MOD * D), 0.02),
        'norm_g': 1.0 + nrm(ks[8], (DEPTH, 3, D), 0.02),
        'ffn_w13': nrm(ks[9], (DEPTH, 2, D, 2 * D_FF), D ** -0.5),
        'ffn_w2': nrm(ks[10], (DEPTH, 2, D_FF, D), D_FF ** -0.5),
        'w_in': nrm(ks[11], (DEPTH, D, IN_WIDTH), D ** -0.5),
        'q_norm_g': 1.0 + nrm(ks[12], (DEPTH, HEAD_DIM), 0.02),
        'k_norm_g': 1.0 + nrm(ks[13], (DEPTH, HEAD_DIM), 0.02),
        'lam_qk': nrm(ks[14], (DEPTH, 4, HEAD_DIM), 0.1),
        'subln_g': 1.0 + nrm(ks[15], (DEPTH, V_DIM), 0.02),
        'w_pool': nrm(ks[16], (DEPTH, len(POOL_WINDOWS), POOL_GROUP, POOL_GROUP), POOL_GROUP ** -0.5),
        'pool_scale': 1.0 + nrm(ks[17], (DEPTH, POOL_WIDTH), 0.02),
        'w_gate': nrm(ks[18], (DEPTH, D, 3 * D), D ** -0.5),
        'w_pa': nrm(ks[19], (DEPTH, ATTN_WIDTH, D), ATTN_WIDTH ** -0.5),
        'w_pp': nrm(ks[20], (DEPTH, POOL_WIDTH, D), POOL_WIDTH ** -0.5),
        'w_pf': nrm(ks[21], (DEPTH, FOURIER_WIDTH, D), FOURIER_WIDTH ** -0.5),
        'w_out': nrm(ks[22], (DEPTH, D, D), D ** -0.5),
    }


def reference(x_prompt, x_sample, cache_k, cache_v, c, c_ctx, w_ada, b_ada, norm_g, ffn_w13, ffn_w2,
              w_in, q_norm_g, k_norm_g, lam_qk, subln_g, w_pool, pool_scale, w_gate, w_pa, w_pp, w_pf, w_out):
    cond_ctx = c_ctx[None]
    xp, xs = x_prompt, x_sample
    new_k, new_v = [], []
    for l in range(DEPTH):
        lam_init = 0.8 - 0.6 * math.exp(-0.3 * l)
        xp, k_l, v_l = trunk_layer(
            xp, cond_ctx, None, None, lam_init, w_ada[l], b_ada[l], norm_g[l], ffn_w13[l], ffn_w2[l],
            w_in[l], q_norm_g[l], k_norm_g[l], lam_qk[l], subln_g[l], w_pool[l], pool_scale[l],
            w_gate[l], w_pa[l], w_pp[l], w_pf[l], w_out[l])
        new_k.append(k_l)
        new_v.append(v_l)
        xs, _, _ = trunk_layer(
            xs, c, cache_k[:, l], cache_v[:, l], lam_init, w_ada[l], b_ada[l], norm_g[l], ffn_w13[l],
            ffn_w2[l], w_in[l], q_norm_g[l], k_norm_g[l], lam_qk[l], subln_g[l], w_pool[l], pool_scale[l],
            w_gate[l], w_pa[l], w_pp[l], w_pf[l], w_out[l])
    state_k = jnp.stack(new_k, axis=1)
    state_v = jnp.stack(new_v, axis=1)
    return (xp, xs, state_k, state_v)
```

```python
import functools
import math

import numpy as np
import jax
import jax.numpy as jnp
from jax import lax
from jax.experimental import pallas as pl
from jax.experimental.pallas import tpu as pltpu

D_MODEL = 1024
N_HEADS = 8
HEAD_DIM = 64
V_DIM = 2 * HEAD_DIM
QK_WIDTH = N_HEADS * 2 * HEAD_DIM
ATTN_WIDTH = N_HEADS * V_DIM
POOL_WINDOWS = (2, 4, 8, 16)
POOL_GROUP = 128
POOL_WIDTH = POOL_GROUP * len(POOL_WINDOWS)
FOURIER_GROUPS = 4
FOURIER_GROUP = 128
FOURIER_WIDTH = FOURIER_GROUPS * FOURIER_GROUP
D_FF = 2816
N_MOD = 9
GRID_W = 64
ROPE_BASE = 10000.0
EPS = 1e-6

LANES = 128
VMEM_LIMIT_BYTES = 56 * 1024 * 1024

MOD_ROWS = 16
MOD_COL_TILE = 1152
FF_CHUNK = 256
TOKEN_TILE = 512
Q_TILE = 256
DFT_ROW_TILE = 512

F32 = jnp.float32
BF16 = jnp.bfloat16


def _dot(a, b):
    return jnp.dot(a, b, preferred_element_type=F32)


def _dot_nt(a, b):
    return lax.dot_general(a, b, (((1,), (1,)), ((), ())), preferred_element_type=F32)


def _resident(block_shape, index_map):
    return pl.BlockSpec(block_shape, index_map, pipeline_mode=pl.Buffered(1))


def _params(n_axes):
    return pltpu.CompilerParams(
        dimension_semantics=("arbitrary",) * n_axes, vmem_limit_bytes=VMEM_LIMIT_BYTES)


def _modulated_norm(x, gain, mod_ref, first):
    y = x * lax.rsqrt(jnp.mean(x * x, axis=-1, keepdims=True) + EPS) * gain
    return y * (1.0 + mod_ref[first + 1:first + 2, :]) + mod_ref[first:first + 1, :]


@functools.lru_cache(maxsize=None)
def _rope_tables(seq):
    n_freq = HEAD_DIM // 4
    lane = np.arange(LANES)
    within = lane % HEAD_DIM
    axis = within // (2 * n_freq)
    half = (within % (2 * n_freq)) // n_freq
    freq = lane % n_freq
    inv = ROPE_BASE ** (-np.arange(n_freq, dtype=np.float64) / n_freq)
    pos = np.arange(seq)
    coord = np.where(axis[None, :] == 0, (pos // GRID_W)[:, None], (pos % GRID_W)[:, None])
    ang = coord.astype(np.float64) * inv[freq][None, :]
    cos, sin = np.cos(ang), np.sin(ang)
    sin_first = np.where(half[None, :] == 0, -sin, 0.0)
    sin_second = np.where(half[None, :] == 1, sin, 0.0)
    return tuple(np.asarray(t, np.float32) for t in (cos, sin_first, sin_second))


@functools.lru_cache(maxsize=None)
def _segment_mean_matrix():
    idx = np.arange(QK_WIDTH) // HEAD_DIM
    return np.asarray((idx[:, None] == idx[None, :]) / HEAD_DIM, np.float32)


@functools.lru_cache(maxsize=None)
def _dft_tables(seq):
    def cs(n):
        k = np.arange(n, dtype=np.int64)
        ang = 2.0 * np.pi * ((k[:, None] * k[None, :]) % n).astype(np.float64) / n
        return np.cos(ang), np.sin(ang)
    cl, sl = cs(seq)
    cc, sc = cs(FOURIER_GROUP)
    return (np.asarray(np.concatenate([cl, -sl], axis=1), np.float32),
            np.asarray(np.concatenate([cc, sc], axis=1), np.float32))


def _mod_kernel(c_ref, w_ref, b_ref, o_ref):
    c = c_ref[...]
    s = c * jax.nn.sigmoid(c)
    s_hi = s.astype(BF16)
    s_lo = (s - s_hi.astype(F32)).astype(BF16)
    w = w_ref[...]
    w_hi = w.astype(BF16)
    w_lo = (w - w_hi.astype(F32)).astype(BF16)
    o_ref[...] = _dot(s_hi, w_hi) + _dot(s_lo, w_hi) + _dot(s_hi, w_lo) + b_ref[...]


def _modulation(cond, w_ada, b_ada):
    depth = w_ada.shape[0]
    width = N_MOD * D_MODEL
    out = pl.pallas_call(
        _mod_kernel,
        out_shape=jax.ShapeDtypeStruct((depth, MOD_ROWS, width), F32),
        grid=(depth, width // MOD_COL_TILE),
        in_specs=[
            pl.BlockSpec((MOD_ROWS, D_MODEL), lambda l, n: (0, 0)),
            pl.BlockSpec((None, D_MODEL, MOD_COL_TILE), lambda l, n: (l, 0, n)),
            pl.BlockSpec((None, 1, MOD_COL_TILE), lambda l, n: (l, 0, n)),
        ],
        out_specs=pl.BlockSpec((None, MOD_ROWS, MOD_COL_TILE), lambda l, n: (l, 0, n)),
        compiler_params=_params(2),
        name="adaln_modulation",
    )(cond, w_ada, b_ada.reshape(depth, 1, width))
    return out.reshape(depth, MOD_ROWS, N_MOD, D_MODEL)


def _ffn_kernel(x_ref, mod_ref, g_ref, w1_ref, w3_ref, w2_ref, o_ref, h_scr, acc_scr, *, mod_first):
    x = x_ref[...]
    h_scr[...] = _modulated_norm(x, g_ref[...], mod_ref, mod_first).astype(BF16)
    acc_scr[...] = jnp.zeros_like(acc_scr)

    def chunk(j, carry):
        h = h_scr[...]
        gate = _dot(h, w1_ref[j])
        up = _dot(h, w3_ref[j])
        act = (gate * jax.nn.sigmoid(gate)) * up
        acc_scr[...] += _dot(act.astype(BF16), w2_ref[j])
        return carry

    lax.fori_loop(0, D_FF // FF_CHUNK, chunk, 0)
    o_ref[...] = x + (0.5 * mod_ref[mod_first + 2:mod_first + 3, :]) * acc_scr[...]


def _ffn(x, mod_l, mod_row, gain, w1, w3, w2, *, mod_first):
    n = x.shape[0]
    n_chunks = D_FF // FF_CHUNK
    return pl.pallas_call(
        functools.partial(_ffn_kernel, mod_first=mod_first),
        out_shape=jax.ShapeDtypeStruct((n, D_MODEL), F32),
        grid=(n // TOKEN_TILE,),
        in_specs=[
            pl.BlockSpec((TOKEN_TILE, D_MODEL), lambda i: (i, 0)),
            pl.BlockSpec((None, N_MOD, D_MODEL), lambda i: (mod_row(i), 0, 0)),
            pl.BlockSpec((1, D_MODEL), lambda i: (0, 0)),
            _resident((n_chunks, D_MODEL, FF_CHUNK), lambda i: (0, 0, 0)),
            _resident((n_chunks, D_MODEL, FF_CHUNK), lambda i: (0, 0, 0)),
            _resident((n_chunks, FF_CHUNK, D_MODEL), lambda i: (0, 0, 0)),
        ],
        out_specs=pl.BlockSpec((TOKEN_TILE, D_MODEL), lambda i: (i, 0)),
        scratch_shapes=[pltpu.VMEM((TOKEN_TILE, D_MODEL), BF16),
                        pltpu.VMEM((TOKEN_TILE, D_MODEL), F32)],
        compiler_params=_params(1),
        name="swiglu_ffn",
    )(x, mod_l, gain, w1, w3, w2)


def _mixer_in_kernel(*refs, rope, emit_state):
    refs = list(refs)
    x_ref, mod_ref, g_ref, win_ref, seg_ref, gq_ref, gk_ref = refs[:7]
    refs = refs[7:]
    if rope:
        cos_ref, sin_first_ref, sin_second_ref = refs[:3]
        refs = refs[3:]
    h_ref, q_ref, k_ref, v_ref, up_ref, uf_ref = refs[:6]
    refs = refs[6:]
    if emit_state:
        sk_ref, sv_ref = refs

    h = _modulated_norm(x_ref[...], g_ref[...], mod_ref, 3).astype(BF16)
    h_ref[...] = h

    def head_norm(t, gain):
        mean_sq = _dot((t * t).astype(BF16), seg_ref[...])
        return t * lax.rsqrt(mean_sq + EPS) * gain

    def head_chunks(t):
        return [t[:, LANES * i:LANES * (i + 1)] for i in range(N_HEADS)]

    def rotate(chunk):
        return (chunk * cos_ref[...]
                + pltpu.roll(chunk, LANES - HEAD_DIM // 4, axis=1) * sin_first_ref[...]
                + pltpu.roll(chunk, HEAD_DIM // 4, axis=1) * sin_second_ref[...])

    q = head_norm(_dot(h, win_ref[:, 0:QK_WIDTH]), gq_ref[...])
    for i, chunk in enumerate(head_chunks(q)):
        if rope:
            chunk = rotate(chunk)
        q_ref[:, LANES * i:LANES * (i + 1)] = (chunk * (HEAD_DIM ** -0.5)).astype(BF16)

    k = head_norm(_dot(h, win_ref[:, QK_WIDTH:2 * QK_WIDTH]), gk_ref[...])
    if emit_state:
        sk_ref[...] = k
    for i, chunk in enumerate(head_chunks(k)):
        if rope:
            chunk = rotate(chunk)
        k_ref[:, LANES * i:LANES * (i + 1)] = chunk.astype(BF16)

    v0 = 2 * QK_WIDTH
    v = _dot(h, win_ref[:, v0:v0 + ATTN_WIDTH])
    if emit_state:
        sv_ref[...] = v
    v_ref[...] = v.astype(BF16)

    p0 = v0 + ATTN_WIDTH
    up_ref[...] = _dot(h, win_ref[:, p0:p0 + POOL_WIDTH])
    f0 = p0 + POOL_WIDTH
    uf_ref[...] = _dot(h, win_ref[:, f0:f0 + FOURIER_WIDTH]).astype(BF16)


def _mixer_in(x, mod_l, mod_row, gain, w_in, seg, gq, gk, rope_tables, *, seq, emit_state):
    n = x.shape[0]
    rope = rope_tables is not None
    tiles_per_seq = seq // TOKEN_TILE
    tok = lambda width: pl.BlockSpec((TOKEN_TILE, width), lambda i: (i, 0))
    in_specs = [
        tok(D_MODEL),
        pl.BlockSpec((None, N_MOD, D_MODEL), lambda i: (mod_row(i), 0, 0)),
        pl.BlockSpec((1, D_MODEL), lambda i: (0, 0)),
        _resident(w_in.shape, lambda i: (0, 0)),
        _resident(seg.shape, lambda i: (0, 0)),
        pl.BlockSpec((1, QK_WIDTH), lambda i: (0, 0)),
        pl.BlockSpec((1, QK_WIDTH), lambda i: (0, 0)),
    ]
    args = [x, mod_l, gain, w_in, seg, gq, gk]
    if rope:
        in_specs += [pl.BlockSpec((TOKEN_TILE, LANES), lambda i: (i % tiles_per_seq, 0))] * 3
        args += list(rope_tables)
    out_shape = [jax.ShapeDtypeStruct((n, D_MODEL), BF16),
                 jax.ShapeDtypeStruct((n, QK_WIDTH), BF16),
                 jax.ShapeDtypeStruct((n, QK_WIDTH), BF16),
                 jax.ShapeDtypeStruct((n, ATTN_WIDTH), BF16),
                 jax.ShapeDtypeStruct((n, POOL_WIDTH), F32),
                 jax.ShapeDtypeStruct((n, FOURIER_WIDTH), BF16)]
    out_specs = [tok(D_MODEL), tok(QK_WIDTH), tok(QK_WIDTH), tok(ATTN_WIDTH),
                 tok(POOL_WIDTH), tok(FOURIER_WIDTH)]
    if emit_state:
        out_shape += [jax.ShapeDtypeStruct((n, QK_WIDTH), F32),
                      jax.ShapeDtypeStruct((n, ATTN_WIDTH), F32)]
        out_specs += [tok(QK_WIDTH), tok(ATTN_WIDTH)]
    return pl.pallas_call(
        functools.partial(_mixer_in_kernel, rope=rope, emit_state=emit_state),
        out_shape=out_shape,
        grid=(n // TOKEN_TILE,),
        in_specs=in_specs,
        out_specs=out_specs,
        compiler_params=_params(1),
        name="mixer_in",
    )(*args)


def _attention_kernel(*refs, n_past, lam_init):
    refs = list(refs)
    q_ref, k_ref, v_ref = refs[:3]
    refs = refs[3:]
    if n_past:
        ck_ref, cv_ref = refs[:2]
        refs = refs[2:]
    lam_ref, gain_ref, o_ref, k_all, v_ext = refs

    @pl.when(pl.program_id(2) == 0)
    def _():
        if n_past:
            k_all[0:n_past, :] = ck_ref[...].astype(BF16)
            v_ext[0:n_past, 0:V_DIM] = cv_ref[...].astype(BF16)
        k_all[n_past:, :] = k_ref[...]
        v_ext[n_past:, 0:V_DIM] = v_ref[...]
        v_ext[:, V_DIM:] = jnp.ones((v_ext.shape[0], V_DIM), BF16)

    q = q_ref[...]
    lane = lax.broadcasted_iota(jnp.int32, q.shape, 1)
    zero = jnp.zeros_like(q)

    def softmax_times_v(q_half):
        s = _dot_nt(q_half, k_all[...])
        e = jnp.exp(s - jnp.max(s, axis=-1, keepdims=True)).astype(BF16)
        ov = _dot(e, v_ext[...])
        return ov[:, 0:V_DIM] / ov[:, V_DIM:V_DIM + 1]

    o1 = softmax_times_v(jnp.where(lane < HEAD_DIM, q, zero))
    o2 = softmax_times_v(jnp.where(lane >= HEAD_DIM, q, zero))

    lq = lam_ref[...]
    lam = (jnp.exp(jnp.sum(lq[0:1, :] * lq[1:2, :], axis=-1, keepdims=True))
           - jnp.exp(jnp.sum(lq[2:3, :] * lq[3:4, :], axis=-1, keepdims=True)) + lam_init)
    o = o1 - lam * o2
    o = o * lax.rsqrt(jnp.mean(o * o, axis=-1, keepdims=True) + EPS) * gain_ref[...]
    o_ref[...] = (o * (1.0 - lam_init)).astype(BF16)


def _attention(q, k, v, cache_k, cache_v, layer, lam_qk, subln_gain, *, lam_init):
    batch, seq, _ = q.shape
    n_past = 0 if cache_k is None else cache_k.shape[2]
    q_tile = min(Q_TILE, seq)
    head_block = lambda rows: pl.BlockSpec((None, rows, LANES), lambda b, h, i: (b, 0, h))
    in_specs = [pl.BlockSpec((None, q_tile, LANES), lambda b, h, i: (b, i, h)),
                head_block(seq), head_block(seq)]
    args = [q, k, v]
    if n_past:
        past_block = pl.BlockSpec((None, None, n_past, LANES), lambda b, h, i: (b, layer, 0, h))
        in_specs += [past_block, past_block]
        args += [cache_k, cache_v]
    in_specs += [pl.BlockSpec((4, HEAD_DIM), lambda b, h, i: (0, 0)),
                 pl.BlockSpec((1, V_DIM), lambda b, h, i: (0, 0))]
    args += [lam_qk, subln_gain]
    return pl.pallas_call(
        functools.partial(_attention_kernel, n_past=n_past, lam_init=lam_init),
        out_shape=jax.ShapeDtypeStruct((batch, seq, ATTN_WIDTH), BF16),
        grid=(batch, N_HEADS, seq // q_tile),
        in_specs=in_specs,
        out_specs=pl.BlockSpec((None, q_tile, LANES), lambda b, h, i: (b, i, h)),
        scratch_shapes=[pltpu.VMEM((n_past + seq, LANES), BF16),
                        pltpu.VMEM((n_past + seq, 2 * V_DIM), BF16)],
        compiler_params=_params(3),
        name="diff_attention",
    )(*args)


def _pool_kernel(u_ref, w_ref, scale_ref, o_ref):
    seq = u_ref.shape[0]
    row = lax.broadcasted_iota(jnp.int32, (seq, POOL_GROUP), 0)
    for g, window in enumerate(POOL_WINDOWS):
        half = window // 2
        lanes = slice(POOL_GROUP * g, POOL_GROUP * (g + 1))
        u = u_ref[:, lanes]
        win_sum = u
        for offset in range(-half, half):
            if offset == 0:
                continue
            shifted = pltpu.roll(u, (-offset) % seq, axis=0)
            src = row + offset
            win_sum = win_sum + jnp.where((src >= 0) & (src < seq), shifted, 0.0)
        count = (jnp.minimum(row + half, seq) - jnp.maximum(row - half, 0)).astype(F32)
        d = win_sum / count - u
        o_ref[:, lanes] = (_dot(d.astype(BF16), w_ref[g]) * scale_ref[:, lanes]).astype(BF16)


def _pool(u, w_pool, scale):
    batch, seq, _ = u.shape
    return pl.pallas_call(
        _pool_kernel,
        out_shape=jax.ShapeDtypeStruct((batch, seq, POOL_WIDTH), BF16),
        grid=(batch,),
        in_specs=[pl.BlockSpec((None, seq, POOL_WIDTH), lambda b: (b, 0, 0)),
                  pl.BlockSpec(w_pool.shape, lambda b: (0, 0, 0)),
                  pl.BlockSpec((1, POOL_WIDTH), lambda b: (0, 0))],
        out_specs=pl.BlockSpec((None, seq, POOL_WIDTH), lambda b: (b, 0, 0)),
        compiler_params=_params(1),
        name="pool_mix",
    )(u, w_pool, scale)


def _fourier_kernel(x_ref, chan_ref, pos_ref, o_ref, t_scr):
    seq = x_ref.shape[0]

    @pl.when(pl.program_id(1) == 0)
    def _():
        for g in range(FOURIER_GROUPS):
            lanes = slice(FOURIER_GROUP * g, FOURIER_GROUP * (g + 1))
            t = _dot(x_ref[:, lanes], chan_ref[...])
            t_scr[0:seq, lanes] = t[:, 0:FOURIER_GROUP].astype(BF16)
            t_scr[seq:2 * seq, lanes] = t[:, FOURIER_GROUP:].astype(BF16)

    scale = 1.0 / math.sqrt(seq * FOURIER_GROUP)
    o_ref[...] = (_dot(pos_ref[...], t_scr[...]) * scale).astype(BF16)


def _fourier(x, pos_table, chan_table):
    batch, seq, _ = x.shape
    row_tile = min(DFT_ROW_TILE, seq)
    return pl.pallas_call(
        _fourier_kernel,
        out_shape=jax.ShapeDtypeStruct((batch, seq, FOURIER_WIDTH), BF16),
        grid=(batch, seq // row_tile),
        in_specs=[pl.BlockSpec((None, seq, FOURIER_WIDTH), lambda b, t: (b, 0, 0)),
                  pl.BlockSpec(chan_table.shape, lambda b, t: (0, 0)),
                  pl.BlockSpec((row_tile, 2 * seq), lambda b, t: (t, 0))],
        out_specs=pl.BlockSpec((None, row_tile, FOURIER_WIDTH), lambda b, t: (b, t, 0)),
        scratch_shapes=[pltpu.VMEM((2 * seq, FOURIER_WIDTH), BF16)],
        compiler_params=_params(2),
        name="fourier_mix",
    )(x, chan_table, pos_table)


def _mixer_out_kernel(x_ref, mod_ref, h_ref, a_ref, p_ref, f_ref,
                      wg_ref, wpa_ref, wpp_ref, wpf_ref, wo_ref, o_ref):
    h = h_ref[...]

    def gated(branch_ref, w_ref, index):
        gate = jax.nn.sigmoid(_dot(h, wg_ref[:, D_MODEL * index:D_MODEL * (index + 1)]))
        return gate * _dot(branch_ref[...], w_ref[...])

    m = gated(a_ref, wpa_ref, 0) + gated(p_ref, wpp_ref, 1) + gated(f_ref, wpf_ref, 2)
    y = _dot(m.astype(BF16), wo_ref[...])
    o_ref[...] = x_ref[...] + mod_ref[5:6, :] * y


def _mixer_out(x, mod_l, mod_row, h, a, p, f, w_gate, w_pa, w_pp, w_pf, w_out):
    n = x.shape[0]
    tok = lambda width: pl.BlockSpec((TOKEN_TILE, width), lambda i: (i, 0))
    weights = [w_gate, w_pa, w_pp, w_pf, w_out]
    return pl.pallas_call(
        _mixer_out_kernel,
        out_shape=jax.ShapeDtypeStruct((n, D_MODEL), F32),
        grid=(n // TOKEN_TILE,),
        in_specs=[tok(D_MODEL),
                  pl.BlockSpec((None, N_MOD, D_MODEL), lambda i: (mod_row(i), 0, 0)),
                  tok(D_MODEL), tok(ATTN_WIDTH), tok(POOL_WIDTH), tok(FOURIER_WIDTH)]
                 + [_resident(w.shape, lambda i: (0, 0)) for w in weights],
        out_specs=tok(D_MODEL),
        compiler_params=_params(1),
        name="mixer_out",
    )(x, mod_l, h, a, p, f, *weights)


def _trunk_layer(x, batch, seq, mod_l, mod_row, cache, layer, lam_init, w, tables):
    x = _ffn(x, mod_l, mod_row, w["norm_g"][0], *w["ffn"][0], mod_first=0)

    emit_state = cache is None
    outs = _mixer_in(x, mod_l, mod_row, w["norm_g"][1], w["w_in"], tables["seg"], w["gq"], w["gk"],
                     None if emit_state else tables["rope"], seq=seq, emit_state=emit_state)
    h, q, k, v, up, uf = outs[:6]
    per_seq = lambda t: t.reshape(batch, seq, t.shape[-1])
    cache_k, cache_v = (None, None) if emit_state else cache
    a = _attention(per_seq(q), per_seq(k), per_seq(v), cache_k, cache_v, layer,
                   w["lam_qk"], w["subln_g"], lam_init=lam_init)
    p = _pool(per_seq(up), w["w_pool"], w["pool_scale"])
    f = _fourier(per_seq(uf), *tables["dft"])
    flat = lambda t: t.reshape(batch * seq, t.shape[-1])
    x = _mixer_out(x, mod_l, mod_row, h, flat(a), flat(p), flat(f),
                   w["w_gate"], w["w_pa"], w["w_pp"], w["w_pf"], w["w_out"])

    x = _ffn(x, mod_l, mod_row, w["norm_g"][2], *w["ffn"][1], mod_first=6)
    state = (outs[6], outs[7]) if emit_state else (None, None)
    return x, state


def kernel(x_prompt, x_sample, cache_k, cache_v, c, c_ctx, w_ada, b_ada, norm_g, ffn_w13, ffn_w2, w_in, q_norm_g, k_norm_g, lam_qk, subln_g, w_pool, pool_scale, w_gate, w_pa, w_pp, w_pf, w_out):
    batch, seq, _ = x_prompt.shape
    dec_batch, dec_seq, _ = x_sample.shape
    depth = w_ada.shape[0]
    n_past = cache_k.shape[2]
    assert dec_batch + 1 <= MOD_ROWS
    assert seq % TOKEN_TILE == 0 or TOKEN_TILE % seq == 0
    assert dec_seq % TOKEN_TILE == 0 and (batch * seq) % TOKEN_TILE == 0

    cond = jnp.concatenate(
        [c, c_ctx[None], jnp.zeros((MOD_ROWS - dec_batch - 1, D_MODEL), F32)], axis=0)
    mod = _modulation(cond, w_ada, b_ada)

    n_chunks = D_FF // FF_CHUNK
    w13 = ffn_w13.astype(BF16).reshape(depth, 2, D_MODEL, 2, n_chunks, FF_CHUNK)
    w13 = jnp.transpose(w13, (0, 1, 3, 4, 2, 5))
    w2 = ffn_w2.astype(BF16).reshape(depth, 2, n_chunks, FF_CHUNK, D_MODEL)
    cast = lambda t: t.astype(BF16)
    w_in_b, w_gate_b, w_pa_b, w_pp_b, w_pf_b, w_out_b, w_pool_b = map(
        cast, (w_in, w_gate, w_pa, w_pp, w_pf, w_out, w_pool))
    tile_gain = lambda g: jnp.tile(g, QK_WIDTH // HEAD_DIM)[None, :]

    seg = jnp.asarray(_segment_mean_matrix()).astype(BF16)
    tables_ctx = {"seg": seg, "rope": None,
                  "dft": tuple(jnp.asarray(t).astype(BF16) for t in _dft_tables(seq))}
    tables_lat = {"seg": seg, "rope": tuple(jnp.asarray(t) for t in _rope_tables(dec_seq)),
                  "dft": tuple(jnp.asarray(t).astype(BF16) for t in _dft_tables(dec_seq))}
    cache = (cache_k.reshape(dec_batch, depth, n_past, QK_WIDTH),
             cache_v.reshape(dec_batch, depth, n_past, ATTN_WIDTH))

    ctx_row = lambda i: dec_batch
    lat_tiles = dec_seq // TOKEN_TILE
    lat_row = lambda i: i // lat_tiles

    xp = x_prompt.reshape(batch * seq, D_MODEL)
    xs = x_sample.reshape(dec_batch * dec_seq, D_MODEL)
    new_k, new_v = [], []
    for l in range(depth):
        lam_init = 0.8 - 0.6 * math.exp(-0.3 * l)
        w = {
            "norm_g": [norm_g[l, s][None, :] for s in range(3)],
            "ffn": [(w13[l, s, 0], w13[l, s, 1], w2[l, s]) for s in range(2)],
            "w_in": w_in_b[l], "gq": tile_gain(q_norm_g[l]), "gk": tile_gain(k_norm_g[l]),
            "lam_qk": lam_qk[l], "subln_g": subln_g[l][None, :],
            "w_pool": w_pool_b[l], "pool_scale": pool_scale[l][None, :],
            "w_gate": w_gate_b[l], "w_pa": w_pa_b[l], "w_pp": w_pp_b[l], "w_pf": w_pf_b[l],
            "w_out": w_out_b[l],
        }
        xp, (k_l, v_l) = _trunk_layer(xp, batch, seq, mod[l], ctx_row, None, l, lam_init, w, tables_ctx)
        new_k.append(k_l.reshape(batch, seq, N_HEADS, 2 * HEAD_DIM))
        new_v.append(v_l.reshape(batch, seq, N_HEADS, V_DIM))
        xs, _ = _trunk_layer(xs, dec_batch, dec_seq, mod[l], lat_row, cache, l, lam_init, w, tables_lat)

    return (xp.reshape(batch, seq, D_MODEL), xs.reshape(dec_batch, dec_seq, D_MODEL),
            jnp.stack(new_k, axis=1), jnp.stack(new_v, axis=1))
```

```python
import functools
import math

import numpy as np
import jax
import jax.numpy as jnp
from jax import lax
from jax.experimental import pallas as pl
from jax.experimental.pallas import tpu as pltpu

D_MODEL = 1024
N_HEADS = 8
HEAD_DIM = 64
V_DIM = 2 * HEAD_DIM
QK_WIDTH = N_HEADS * 2 * HEAD_DIM
ATTN_WIDTH = N_HEADS * V_DIM
POOL_WINDOWS = (2, 4, 8, 16)
POOL_GROUP = 128
POOL_WIDTH = POOL_GROUP * len(POOL_WINDOWS)
FOURIER_GROUPS = 4
FOURIER_GROUP = 128
FOURIER_WIDTH = FOURIER_GROUPS * FOURIER_GROUP
D_FF = 2816
N_MOD = 9
GRID_W = 64
ROPE_BASE = 10000.0
EPS = 1e-6

LANES = 128
VMEM_LIMIT_BYTES = 56 * 1024 * 1024

MOD_ROWS = 16
MOD_COL_TILE = 1152
FF_CHUNK = 256
TOKEN_TILE = 512
Q_TILE = 512
KEY_CHUNK = 512
LOG2_E = math.log2(math.e)
DFT_ROW_TILE = 512

F32 = jnp.float32
BF16 = jnp.bfloat16


def _dot(a, b):
    return jnp.dot(a, b, preferred_element_type=F32)


def _dot_nt(a, b):
    return lax.dot_general(a, b, (((1,), (1,)), ((), ())), preferred_element_type=F32)


def _resident(block_shape, index_map):
    return pl.BlockSpec(block_shape, index_map, pipeline_mode=pl.Buffered(1))


def _params(n_axes):
    return pltpu.CompilerParams(
        dimension_semantics=("arbitrary",) * n_axes, vmem_limit_bytes=VMEM_LIMIT_BYTES)


def _modulated_norm(x, gain, mod_ref, first):
    y = x * lax.rsqrt(jnp.mean(x * x, axis=-1, keepdims=True) + EPS) * gain
    return y * (1.0 + mod_ref[first + 1:first + 2, :]) + mod_ref[first:first + 1, :]


@functools.lru_cache(maxsize=None)
def _rope_tables(seq):
    n_freq = HEAD_DIM // 4
    lane = np.arange(LANES)
    within = lane % HEAD_DIM
    axis = within // (2 * n_freq)
    half = (within % (2 * n_freq)) // n_freq
    freq = lane % n_freq
    inv = ROPE_BASE ** (-np.arange(n_freq, dtype=np.float64) / n_freq)
    pos = np.arange(seq)
    coord = np.where(axis[None, :] == 0, (pos // GRID_W)[:, None], (pos % GRID_W)[:, None])
    ang = coord.astype(np.float64) * inv[freq][None, :]
    cos, sin = np.cos(ang), np.sin(ang)
    sin_first = np.where(half[None, :] == 0, -sin, 0.0)
    sin_second = np.where(half[None, :] == 1, sin, 0.0)
    return tuple(np.asarray(t, np.float32) for t in (cos, sin_first, sin_second))


@functools.lru_cache(maxsize=None)
def _segment_mean_matrix():
    idx = np.arange(QK_WIDTH) // HEAD_DIM
    return np.asarray((idx[:, None] == idx[None, :]) / HEAD_DIM, np.float32)


@functools.lru_cache(maxsize=None)
def _dft_tables(seq):
    def cs(n):
        k = np.arange(n, dtype=np.int64)
        ang = 2.0 * np.pi * ((k[:, None] * k[None, :]) % n).astype(np.float64) / n
        return np.cos(ang), np.sin(ang)
    cl, sl = cs(seq)
    cc, sc = cs(FOURIER_GROUP)
    return (np.asarray(np.concatenate([cl, -sl], axis=1), np.float32),
            np.asarray(np.concatenate([cc, sc], axis=1), np.float32))


def _mod_kernel(c_ref, w_ref, b_ref, o_ref):
    c = c_ref[...]
    s = c * jax.nn.sigmoid(c)
    s_hi = s.astype(BF16)
    s_lo = (s - s_hi.astype(F32)).astype(BF16)
    w = w_ref[...]
    w_hi = w.astype(BF16)
    w_lo = (w - w_hi.astype(F32)).astype(BF16)
    o_ref[...] = _dot(s_hi, w_hi) + _dot(s_lo, w_hi) + _dot(s_hi, w_lo) + b_ref[...]


def _modulation(cond, w_ada, b_ada):
    depth = w_ada.shape[0]
    width = N_MOD * D_MODEL
    out = pl.pallas_call(
        _mod_kernel,
        out_shape=jax.ShapeDtypeStruct((depth, MOD_ROWS, width), F32),
        grid=(depth, width // MOD_COL_TILE),
        in_specs=[
            pl.BlockSpec((MOD_ROWS, D_MODEL), lambda l, n: (0, 0)),
            pl.BlockSpec((None, D_MODEL, MOD_COL_TILE), lambda l, n: (l, 0, n)),
            pl.BlockSpec((None, 1, MOD_COL_TILE), lambda l, n: (l, 0, n)),
        ],
        out_specs=pl.BlockSpec((None, MOD_ROWS, MOD_COL_TILE), lambda l, n: (l, 0, n)),
        compiler_params=_params(2),
        name="adaln_modulation",
    )(cond, w_ada, b_ada.reshape(depth, 1, width))
    return out.reshape(depth, MOD_ROWS, N_MOD, D_MODEL)


def _ffn_kernel(x_ref, mod_ref, g_ref, w1_ref, w3_ref, w2_ref, o_ref, h_scr, acc_scr, *, mod_first):
    x = x_ref[...]
    h_scr[...] = _modulated_norm(x, g_ref[...], mod_ref, mod_first).astype(BF16)
    acc_scr[...] = jnp.zeros_like(acc_scr)

    def chunk(j, carry):
        h = h_scr[...]
        gate = _dot(h, w1_ref[j])
        up = _dot(h, w3_ref[j])
        act = (gate * jax.nn.sigmoid(gate)) * up
        acc_scr[...] += _dot(act.astype(BF16), w2_ref[j])
        return carry

    lax.fori_loop(0, D_FF // FF_CHUNK, chunk, 0, unroll=True)
    o_ref[...] = x + (0.5 * mod_ref[mod_first + 2:mod_first + 3, :]) * acc_scr[...]


def _ffn(x, mod_l, mod_row, gain, w1, w3, w2, *, mod_first):
    n = x.shape[0]
    n_chunks = D_FF // FF_CHUNK
    return pl.pallas_call(
        functools.partial(_ffn_kernel, mod_first=mod_first),
        out_shape=jax.ShapeDtypeStruct((n, D_MODEL), F32),
        grid=(n // TOKEN_TILE,),
        in_specs=[
            pl.BlockSpec((TOKEN_TILE, D_MODEL), lambda i: (i, 0)),
            pl.BlockSpec((None, N_MOD, D_MODEL), lambda i: (mod_row(i), 0, 0)),
            pl.BlockSpec((1, D_MODEL), lambda i: (0, 0)),
            _resident((n_chunks, D_MODEL, FF_CHUNK), lambda i: (0, 0, 0)),
            _resident((n_chunks, D_MODEL, FF_CHUNK), lambda i: (0, 0, 0)),
            _resident((n_chunks, FF_CHUNK, D_MODEL), lambda i: (0, 0, 0)),
        ],
        out_specs=pl.BlockSpec((TOKEN_TILE, D_MODEL), lambda i: (i, 0)),
        scratch_shapes=[pltpu.VMEM((TOKEN_TILE, D_MODEL), BF16),
                        pltpu.VMEM((TOKEN_TILE, D_MODEL), F32)],
        compiler_params=_params(1),
        name="swiglu_ffn",
    )(x, mod_l, gain, w1, w3, w2)


def _mixer_in_kernel(*refs, rope, emit_state):
    refs = list(refs)
    x_ref, mod_ref, g_ref, win_ref, seg_ref, gq_ref, gk_ref = refs[:7]
    refs = refs[7:]
    if rope:
        cos_ref, sin_first_ref, sin_second_ref = refs[:3]
        refs = refs[3:]
    h_ref, q_ref, k_ref, v_ref, up_ref, uf_ref = refs[:6]
    refs = refs[6:]
    if emit_state:
        sk_ref, sv_ref = refs

    h = _modulated_norm(x_ref[...], g_ref[...], mod_ref, 3).astype(BF16)
    h_ref[...] = h

    def head_norm(t, gain):
        mean_sq = _dot((t * t).astype(BF16), seg_ref[...])
        return t * lax.rsqrt(mean_sq + EPS) * gain

    def head_chunks(t):
        return [t[:, LANES * i:LANES * (i + 1)] for i in range(N_HEADS)]

    def rotate(chunk):
        return (chunk * cos_ref[...]
                + pltpu.roll(chunk, LANES - HEAD_DIM // 4, axis=1) * sin_first_ref[...]
                + pltpu.roll(chunk, HEAD_DIM // 4, axis=1) * sin_second_ref[...])

    q = head_norm(_dot(h, win_ref[:, 0:QK_WIDTH]), gq_ref[...])
    for i, chunk in enumerate(head_chunks(q)):
        if rope:
            chunk = rotate(chunk)
        q_ref[:, LANES * i:LANES * (i + 1)] = (chunk * (HEAD_DIM ** -0.5 * LOG2_E)).astype(BF16)

    k = head_norm(_dot(h, win_ref[:, QK_WIDTH:2 * QK_WIDTH]), gk_ref[...])
    if emit_state:
        sk_ref[...] = k
    for i, chunk in enumerate(head_chunks(k)):
        if rope:
            chunk = rotate(chunk)
        k_ref[:, LANES * i:LANES * (i + 1)] = chunk.astype(BF16)

    v0 = 2 * QK_WIDTH
    v = _dot(h, win_ref[:, v0:v0 + ATTN_WIDTH])
    if emit_state:
        sv_ref[...] = v
    v_ref[...] = v.astype(BF16)

    p0 = v0 + ATTN_WIDTH
    up_ref[...] = _dot(h, win_ref[:, p0:p0 + POOL_WIDTH])
    f0 = p0 + POOL_WIDTH
    uf_ref[...] = _dot(h, win_ref[:, f0:f0 + FOURIER_WIDTH]).astype(BF16)


def _mixer_in(x, mod_l, mod_row, gain, w_in, seg, gq, gk, rope_tables, *, seq, emit_state):
    n = x.shape[0]
    rope = rope_tables is not None
    tiles_per_seq = seq // TOKEN_TILE
    tok = lambda width: pl.BlockSpec((TOKEN_TILE, width), lambda i: (i, 0))
    in_specs = [
        tok(D_MODEL),
        pl.BlockSpec((None, N_MOD, D_MODEL), lambda i: (mod_row(i), 0, 0)),
        pl.BlockSpec((1, D_MODEL), lambda i: (0, 0)),
        _resident(w_in.shape, lambda i: (0, 0)),
        _resident(seg.shape, lambda i: (0, 0)),
        pl.BlockSpec((1, QK_WIDTH), lambda i: (0, 0)),
        pl.BlockSpec((1, QK_WIDTH), lambda i: (0, 0)),
    ]
    args = [x, mod_l, gain, w_in, seg, gq, gk]
    if rope:
        in_specs += [pl.BlockSpec((TOKEN_TILE, LANES), lambda i: (i % tiles_per_seq, 0))] * 3
        args += list(rope_tables)
    out_shape = [jax.ShapeDtypeStruct((n, D_MODEL), BF16),
                 jax.ShapeDtypeStruct((n, QK_WIDTH), BF16),
                 jax.ShapeDtypeStruct((n, QK_WIDTH), BF16),
                 jax.ShapeDtypeStruct((n, ATTN_WIDTH), BF16),
                 jax.ShapeDtypeStruct((n, POOL_WIDTH), F32),
                 jax.ShapeDtypeStruct((n, FOURIER_WIDTH), BF16)]
    out_specs = [tok(D_MODEL), tok(QK_WIDTH), tok(QK_WIDTH), tok(ATTN_WIDTH),
                 tok(POOL_WIDTH), tok(FOURIER_WIDTH)]
    if emit_state:
        out_shape += [jax.ShapeDtypeStruct((n, QK_WIDTH), F32),
                      jax.ShapeDtypeStruct((n, ATTN_WIDTH), F32)]
        out_specs += [tok(QK_WIDTH), tok(ATTN_WIDTH)]
    return pl.pallas_call(
        functools.partial(_mixer_in_kernel, rope=rope, emit_state=emit_state),
        out_shape=out_shape,
        grid=(n // TOKEN_TILE,),
        in_specs=in_specs,
        out_specs=out_specs,
        compiler_params=_params(1),
        name="mixer_in",
    )(*args)


def _attention_kernel(*refs, n_past, lam_init):
    refs = list(refs)
    q_ref, k_ref, v_ref = refs[:3]
    refs = refs[3:]
    if n_past:
        ck_ref, cv_ref = refs[:2]
        refs = refs[2:]
    lam_ref, gain_ref, o_ref, k_all, v_all = refs

    @pl.when(pl.program_id(2) == 0)
    def _():
        if n_past:
            k_all[0:n_past, :] = ck_ref[...].astype(BF16)
            v_all[0:n_past, :] = cv_ref[...].astype(BF16)
        k_all[n_past:, :] = k_ref[...]
        v_all[n_past:, :] = v_ref[...]

    q = q_ref[...]
    lane = lax.broadcasted_iota(jnp.int32, q.shape, 1)
    zero = jnp.zeros_like(q)
    n_q = q.shape[0]
    q_both = jnp.concatenate(
        [jnp.where(lane < HEAD_DIM, q, zero), jnp.where(lane >= HEAD_DIM, q, zero)], axis=0)

    n_keys = k_all.shape[0]
    key_chunk = KEY_CHUNK if n_keys % KEY_CHUNK == 0 else n_keys
    run_max = acc = den = None
    for c in range(n_keys // key_chunk):
        keys = slice(key_chunk * c, key_chunk * (c + 1))
        s = _dot_nt(q_both, k_all[keys, :])
        new_max = jnp.max(s, axis=-1, keepdims=True)
        if c:
            new_max = jnp.maximum(run_max, new_max)
        e = jnp.exp2(s - new_max)
        pv = _dot(e.astype(BF16), v_all[keys, :])
        part = e[:, 0:LANES]
        for j in range(1, key_chunk // LANES):
            part = part + e[:, LANES * j:LANES * (j + 1)]
        if c == 0:
            acc, den = pv, part
        else:
            alpha = jnp.exp2(run_max - new_max)
            acc, den = acc * alpha + pv, den * alpha + part
        run_max = new_max
    o_both = acc / jnp.sum(den, axis=-1, keepdims=True)
    o1, o2 = o_both[0:n_q], o_both[n_q:]

    lq = lam_ref[...]
    lam = (jnp.exp(jnp.sum(lq[0:1, :] * lq[1:2, :], axis=-1, keepdims=True))
           - jnp.exp(jnp.sum(lq[2:3, :] * lq[3:4, :], axis=-1, keepdims=True)) + lam_init)
    o = o1 - lam * o2
    o = o * lax.rsqrt(jnp.mean(o * o, axis=-1, keepdims=True) + EPS) * gain_ref[...]
    o_ref[...] = (o * (1.0 - lam_init)).astype(BF16)


def _attention(q, k, v, cache_k, cache_v, layer, lam_qk, subln_gain, *, lam_init):
    batch, seq, _ = q.shape
    n_past = 0 if cache_k is None else cache_k.shape[2]
    q_tile = min(Q_TILE, seq)
    head_block = lambda rows: pl.BlockSpec((None, rows, LANES), lambda b, h, i: (b, 0, h))
    in_specs = [pl.BlockSpec((None, q_tile, LANES), lambda b, h, i: (b, i, h)),
                head_block(seq), head_block(seq)]
    args = [q, k, v]
    if n_past:
        past_block = pl.BlockSpec((None, None, n_past, LANES), lambda b, h, i: (b, layer, 0, h))
        in_specs += [past_block, past_block]
        args += [cache_k, cache_v]
    in_specs += [pl.BlockSpec((4, HEAD_DIM), lambda b, h, i: (0, 0)),
                 pl.BlockSpec((1, V_DIM), lambda b, h, i: (0, 0))]
    args += [lam_qk, subln_gain]
    return pl.pallas_call(
        functools.partial(_attention_kernel, n_past=n_past, lam_init=lam_init),
        out_shape=jax.ShapeDtypeStruct((batch, seq, ATTN_WIDTH), BF16),
        grid=(batch, N_HEADS, seq // q_tile),
        in_specs=in_specs,
        out_specs=pl.BlockSpec((None, q_tile, LANES), lambda b, h, i: (b, i, h)),
        scratch_shapes=[pltpu.VMEM((n_past + seq, LANES), BF16),
                        pltpu.VMEM((n_past + seq, V_DIM), BF16)],
        compiler_params=_params(3),
        name="diff_attention",
    )(*args)


def _pool_kernel(u_ref, w_ref, scale_ref, o_ref):
    seq = u_ref.shape[0]
    row = lax.broadcasted_iota(jnp.int32, (seq, POOL_GROUP), 0)
    for g, window in enumerate(POOL_WINDOWS):
        half = window // 2
        lanes = slice(POOL_GROUP * g, POOL_GROUP * (g + 1))
        u = u_ref[:, lanes]
        win_sum = u
        for offset in range(-half, half):
            if offset == 0:
                continue
            shifted = pltpu.roll(u, (-offset) % seq, axis=0)
            src = row + offset
            win_sum = win_sum + jnp.where((src >= 0) & (src < seq), shifted, 0.0)
        count = (jnp.minimum(row + half, seq) - jnp.maximum(row - half, 0)).astype(F32)
        d = win_sum / count - u
        o_ref[:, lanes] = (_dot(d.astype(BF16), w_ref[g]) * scale_ref[:, lanes]).astype(BF16)


def _pool(u, w_pool, scale):
    batch, seq, _ = u.shape
    return pl.pallas_call(
        _pool_kernel,
        out_shape=jax.ShapeDtypeStruct((batch, seq, POOL_WIDTH), BF16),
        grid=(batch,),
        in_specs=[pl.BlockSpec((None, seq, POOL_WIDTH), lambda b: (b, 0, 0)),
                  pl.BlockSpec(w_pool.shape, lambda b: (0, 0, 0)),
                  pl.BlockSpec((1, POOL_WIDTH), lambda b: (0, 0))],
        out_specs=pl.BlockSpec((None, seq, POOL_WIDTH), lambda b: (b, 0, 0)),
        compiler_params=_params(1),
        name="pool_mix",
    )(u, w_pool, scale)


def _fourier_kernel(x_ref, chan_ref, pos_ref, o_ref, t_scr):
    seq = x_ref.shape[0]

    @pl.when(pl.program_id(1) == 0)
    def _():
        for g in range(FOURIER_GROUPS):
            lanes = slice(FOURIER_GROUP * g, FOURIER_GROUP * (g + 1))
            t = _dot(x_ref[:, lanes], chan_ref[...])
            t_scr[0:seq, lanes] = t[:, 0:FOURIER_GROUP].astype(BF16)
            t_scr[seq:2 * seq, lanes] = t[:, FOURIER_GROUP:].astype(BF16)

    scale = 1.0 / math.sqrt(seq * FOURIER_GROUP)
    o_ref[...] = (_dot(pos_ref[...], t_scr[...]) * scale).astype(BF16)


def _fourier(x, pos_table, chan_table):
    batch, seq, _ = x.shape
    row_tile = min(DFT_ROW_TILE, seq)
    return pl.pallas_call(
        _fourier_kernel,
        out_shape=jax.ShapeDtypeStruct((batch, seq, FOURIER_WIDTH), BF16),
        grid=(batch, seq // row_tile),
        in_specs=[pl.BlockSpec((None, seq, FOURIER_WIDTH), lambda b, t: (b, 0, 0)),
                  pl.BlockSpec(chan_table.shape, lambda b, t: (0, 0)),
                  pl.BlockSpec((row_tile, 2 * seq), lambda b, t: (t, 0))],
        out_specs=pl.BlockSpec((None, row_tile, FOURIER_WIDTH), lambda b, t: (b, t, 0)),
        scratch_shapes=[pltpu.VMEM((2 * seq, FOURIER_WIDTH), BF16)],
        compiler_params=_params(2),
        name="fourier_mix",
    )(x, chan_table, pos_table)


def _mixer_out_kernel(x_ref, mod_ref, h_ref, a_ref, p_ref, f_ref,
                      wg_ref, wpa_ref, wpp_ref, wpf_ref, wo_ref, o_ref):
    h = h_ref[...]

    def gated(branch_ref, w_ref, index):
        gate = jax.nn.sigmoid(_dot(h, wg_ref[:, D_MODEL * index:D_MODEL * (index + 1)]))
        return gate * _dot(branch_ref[...], w_ref[...])

    m = gated(a_ref, wpa_ref, 0) + gated(p_ref, wpp_ref, 1) + gated(f_ref, wpf_ref, 2)
    y = _dot(m.astype(BF16), wo_ref[...])
    o_ref[...] = x_ref[...] + mod_ref[5:6, :] * y


def _mixer_out(x, mod_l, mod_row, h, a, p, f, w_gate, w_pa, w_pp, w_pf, w_out):
    n = x.shape[0]
    tok = lambda width: pl.BlockSpec((TOKEN_TILE, width), lambda i: (i, 0))
    weights = [w_gate, w_pa, w_pp, w_pf, w_out]
    return pl.pallas_call(
        _mixer_out_kernel,
        out_shape=jax.ShapeDtypeStruct((n, D_MODEL), F32),
        grid=(n // TOKEN_TILE,),
        in_specs=[tok(D_MODEL),
                  pl.BlockSpec((None, N_MOD, D_MODEL), lambda i: (mod_row(i), 0, 0)),
                  tok(D_MODEL), tok(ATTN_WIDTH), tok(POOL_WIDTH), tok(FOURIER_WIDTH)]
                 + [_resident(w.shape, lambda i: (0, 0)) for w in weights],
        out_specs=tok(D_MODEL),
        compiler_params=_params(1),
        name="mixer_out",
    )(x, mod_l, h, a, p, f, *weights)


def _trunk_layer(x, batch, seq, mod_l, mod_row, cache, layer, lam_init, w, tables):
    x = _ffn(x, mod_l, mod_row, w["norm_g"][0], *w["ffn"][0], mod_first=0)

    emit_state = cache is None
    outs = _mixer_in(x, mod_l, mod_row, w["norm_g"][1], w["w_in"], tables["seg"], w["gq"], w["gk"],
                     None if emit_state else tables["rope"], seq=seq, emit_state=emit_state)
    h, q, k, v, up, uf = outs[:6]
    per_seq = lambda t: t.reshape(batch, seq, t.shape[-1])
    cache_k, cache_v = (None, None) if emit_state else cache
    a = _attention(per_seq(q), per_seq(k), per_seq(v), cache_k, cache_v, layer,
                   w["lam_qk"], w["subln_g"], lam_init=lam_init)
    p = _pool(per_seq(up), w["w_pool"], w["pool_scale"])
    f = _fourier(per_seq(uf), *tables["dft"])
    flat = lambda t: t.reshape(batch * seq, t.shape[-1])
    x = _mixer_out(x, mod_l, mod_row, h, flat(a), flat(p), flat(f),
                   w["w_gate"], w["w_pa"], w["w_pp"], w["w_pf"], w["w_out"])

    x = _ffn(x, mod_l, mod_row, w["norm_g"][2], *w["ffn"][1], mod_first=6)
    state = (outs[6], outs[7]) if emit_state else (None, None)
    return x, state


def kernel(x_prompt, x_sample, cache_k, cache_v, c, c_ctx, w_ada, b_ada, norm_g, ffn_w13, ffn_w2, w_in, q_norm_g, k_norm_g, lam_qk, subln_g, w_pool, pool_scale, w_gate, w_pa, w_pp, w_pf, w_out):
    batch, seq, _ = x_prompt.shape
    dec_batch, dec_seq, _ = x_sample.shape
    depth = w_ada.shape[0]
    n_past = cache_k.shape[2]
    assert dec_batch + 1 <= MOD_ROWS
    assert seq % TOKEN_TILE == 0 or TOKEN_TILE % seq == 0
    assert dec_seq % TOKEN_TILE == 0 and (batch * seq) % TOKEN_TILE == 0

    cond = jnp.concatenate(
        [c, c_ctx[None], jnp.zeros((MOD_ROWS - dec_batch - 1, D_MODEL), F32)], axis=0)
    mod = _modulation(cond, w_ada, b_ada)

    n_chunks = D_FF // FF_CHUNK
    w13 = ffn_w13.astype(BF16).reshape(depth, 2, D_MODEL, 2, n_chunks, FF_CHUNK)
    w13 = jnp.transpose(w13, (0, 1, 3, 4, 2, 5))
    w2 = ffn_w2.astype(BF16).reshape(depth, 2, n_chunks, FF_CHUNK, D_MODEL)
    cast = lambda t: t.astype(BF16)
    w_in_b, w_gate_b, w_pa_b, w_pp_b, w_pf_b, w_out_b, w_pool_b = map(
        cast, (w_in, w_gate, w_pa, w_pp, w_pf, w_out, w_pool))
    tile_gain = lambda g: jnp.tile(g, QK_WIDTH // HEAD_DIM)[None, :]

    seg = jnp.asarray(_segment_mean_matrix()).astype(BF16)
    tables_ctx = {"seg": seg, "rope": None,
                  "dft": tuple(jnp.asarray(t).astype(BF16) for t in _dft_tables(seq))}
    tables_lat = {"seg": seg, "rope": tuple(jnp.asarray(t) for t in _rope_tables(dec_seq)),
                  "dft": tuple(jnp.asarray(t).astype(BF16) for t in _dft_tables(dec_seq))}
    cache = (cache_k.reshape(dec_batch, depth, n_past, QK_WIDTH),
             cache_v.reshape(dec_batch, depth, n_past, ATTN_WIDTH))

    ctx_row = lambda i: dec_batch
    lat_tiles = dec_seq // TOKEN_TILE
    lat_row = lambda i: i // lat_tiles

    xp = x_prompt.reshape(batch * seq, D_MODEL)
    xs = x_sample.reshape(dec_batch * dec_seq, D_MODEL)
    new_k, new_v = [], []
    for l in range(depth):
        lam_init = 0.8 - 0.6 * math.exp(-0.3 * l)
        w = {
            "norm_g": [norm_g[l, s][None, :] for s in range(3)],
            "ffn": [(w13[l, s, 0], w13[l, s, 1], w2[l, s]) for s in range(2)],
            "w_in": w_in_b[l], "gq": tile_gain(q_norm_g[l]), "gk": tile_gain(k_norm_g[l]),
            "lam_qk": lam_qk[l], "subln_g": subln_g[l][None, :],
            "w_pool": w_pool_b[l], "pool_scale": pool_scale[l][None, :],
            "w_gate": w_gate_b[l], "w_pa": w_pa_b[l], "w_pp": w_pp_b[l], "w_pf": w_pf_b[l],
            "w_out": w_out_b[l],
        }
        xp, (k_l, v_l) = _trunk_layer(xp, batch, seq, mod[l], ctx_row, None, l, lam_init, w, tables_ctx)
        new_k.append(k_l.reshape(batch, seq, N_HEADS, 2 * HEAD_DIM))
        new_v.append(v_l.reshape(batch, seq, N_HEADS, V_DIM))
        xs, _ = _trunk_layer(xs, dec_batch, dec_seq, mod[l], lat_row, cache, l, lam_init, w, tables_lat)

    return (xp.reshape(batch, seq, D_MODEL), xs.reshape(dec_batch, dec_seq, D_MODEL),
            jnp.stack(new_k, axis=1), jnp.stack(new_v, axis=1))
```

```python
import functools
import math

import numpy as np
import jax
import jax.numpy as jnp
from jax import lax
from jax.experimental import pallas as pl
from jax.experimental.pallas import tpu as pltpu

D_MODEL = 1024
N_HEADS = 8
HEAD_DIM = 64
V_DIM = 2 * HEAD_DIM
QK_WIDTH = N_HEADS * 2 * HEAD_DIM
ATTN_WIDTH = N_HEADS * V_DIM
POOL_WINDOWS = (2, 4, 8, 16)
POOL_GROUP = 128
POOL_WIDTH = POOL_GROUP * len(POOL_WINDOWS)
FOURIER_GROUPS = 4
FOURIER_GROUP = 128
FOURIER_WIDTH = FOURIER_GROUPS * FOURIER_GROUP
D_FF = 2816
N_MOD = 9
GRID_W = 64
ROPE_BASE = 10000.0
EPS = 1e-6

LANES = 128
MXU_WIDTH = 256
VMEM_LIMIT_BYTES = 56 * 1024 * 1024

MOD_ROWS = 16
MOD_COL_TILE = 1152
FF_CHUNK = 256
TOKEN_TILE = 512
Q_TILE = 512
KEY_CHUNK = 256
ATTN_KEYS_PER_STEP = 4096
LOG2_E = math.log2(math.e)
DFT_ROW_TILE = 512

F32 = jnp.float32
BF16 = jnp.bfloat16


def _dot(a, b):
    return jnp.dot(a, b, preferred_element_type=F32)


def _dot_nt(a, b):
    return lax.dot_general(a, b, (((1,), (1,)), ((), ())), preferred_element_type=F32)


def _resident(block_shape, index_map):
    return pl.BlockSpec(block_shape, index_map, pipeline_mode=pl.Buffered(1))


def _params(n_axes):
    return pltpu.CompilerParams(
        dimension_semantics=("arbitrary",) * n_axes, vmem_limit_bytes=VMEM_LIMIT_BYTES)


def _modulated_norm(x, gain, mod_ref, first):
    y = x * lax.rsqrt(jnp.mean(x * x, axis=-1, keepdims=True) + EPS) * gain
    return y * (1.0 + mod_ref[first + 1:first + 2, :]) + mod_ref[first:first + 1, :]


@functools.lru_cache(maxsize=None)
def _rope_tables(seq):
    n_freq = HEAD_DIM // 4
    lane = np.arange(LANES)
    within = lane % HEAD_DIM
    axis = within // (2 * n_freq)
    half = (within % (2 * n_freq)) // n_freq
    freq = lane % n_freq
    inv = ROPE_BASE ** (-np.arange(n_freq, dtype=np.float64) / n_freq)
    pos = np.arange(seq)
    coord = np.where(axis[None, :] == 0, (pos // GRID_W)[:, None], (pos % GRID_W)[:, None])
    ang = coord.astype(np.float64) * inv[freq][None, :]
    cos, sin = np.cos(ang), np.sin(ang)
    sin_first = np.where(half[None, :] == 0, -sin, 0.0)
    sin_second = np.where(half[None, :] == 1, sin, 0.0)
    return tuple(np.asarray(t, np.float32) for t in (cos, sin_first, sin_second))


@functools.lru_cache(maxsize=None)
def _segment_mean_matrix():
    idx = np.arange(MXU_WIDTH) // HEAD_DIM
    return np.asarray((idx[:, None] == idx[None, :]) / HEAD_DIM, np.float32)


@functools.lru_cache(maxsize=None)
def _dft_tables(seq):
    def cs(n):
        k = np.arange(n, dtype=np.int64)
        ang = 2.0 * np.pi * ((k[:, None] * k[None, :]) % n).astype(np.float64) / n
        return np.cos(ang), np.sin(ang)
    cl, sl = cs(seq)
    cc, sc = cs(FOURIER_GROUP)
    return (np.asarray(np.concatenate([cl, -sl], axis=1), np.float32),
            np.asarray(np.concatenate([cc, sc], axis=1), np.float32))


def _mod_kernel(c_ref, w_ref, b_ref, o_ref):
    c = c_ref[...]
    s = c * jax.nn.sigmoid(c)
    s_hi = s.astype(BF16)
    s_lo = (s - s_hi.astype(F32)).astype(BF16)
    w = w_ref[...]
    w_hi = w.astype(BF16)
    w_lo = (w - w_hi.astype(F32)).astype(BF16)
    o_ref[...] = _dot(s_hi, w_hi) + _dot(s_lo, w_hi) + _dot(s_hi, w_lo) + b_ref[...]


def _modulation(cond, w_ada, b_ada):
    depth = w_ada.shape[0]
    width = N_MOD * D_MODEL
    out = pl.pallas_call(
        _mod_kernel,
        out_shape=jax.ShapeDtypeStruct((depth, MOD_ROWS, width), F32),
        grid=(depth, width // MOD_COL_TILE),
        in_specs=[
            pl.BlockSpec((MOD_ROWS, D_MODEL), lambda l, n: (0, 0)),
            pl.BlockSpec((None, D_MODEL, MOD_COL_TILE), lambda l, n: (l, 0, n)),
            pl.BlockSpec((None, 1, MOD_COL_TILE), lambda l, n: (l, 0, n)),
        ],
        out_specs=pl.BlockSpec((None, MOD_ROWS, MOD_COL_TILE), lambda l, n: (l, 0, n)),
        compiler_params=_params(2),
        name="adaln_modulation",
    )(cond, w_ada, b_ada.reshape(depth, 1, width))
    return out.reshape(depth, MOD_ROWS, N_MOD, D_MODEL)


def _ffn_kernel(x_ref, mod_ref, g_ref, w13_ref, w2_ref, o_ref, h_scr, acc_scr, *, mod_first):
    x = x_ref[...]
    h_scr[...] = _modulated_norm(x, g_ref[...], mod_ref, mod_first).astype(BF16)
    acc_scr[...] = jnp.zeros_like(acc_scr)

    for j in range(D_FF // FF_CHUNK):
        cols = slice(FF_CHUNK * j, FF_CHUNK * (j + 1))
        h = h_scr[...]
        gate = _dot(h, w13_ref[:, cols])
        up = _dot(h, w13_ref[:, D_FF + FF_CHUNK * j:D_FF + FF_CHUNK * (j + 1)])
        act = (gate * jax.nn.sigmoid(gate)) * up
        acc_scr[...] += _dot(act.astype(BF16), w2_ref[cols, :])

    o_ref[...] = x + (0.5 * mod_ref[mod_first + 2:mod_first + 3, :]) * acc_scr[...]


def _ffn(x, mod_l, mod_row, gain, w13, w2, *, mod_first):
    n = x.shape[0]
    return pl.pallas_call(
        functools.partial(_ffn_kernel, mod_first=mod_first),
        out_shape=jax.ShapeDtypeStruct((n, D_MODEL), F32),
        grid=(n // TOKEN_TILE,),
        in_specs=[
            pl.BlockSpec((TOKEN_TILE, D_MODEL), lambda i: (i, 0)),
            pl.BlockSpec((None, N_MOD, D_MODEL), lambda i: (mod_row(i), 0, 0)),
            pl.BlockSpec((1, D_MODEL), lambda i: (0, 0)),
            _resident((D_MODEL, 2 * D_FF), lambda i: (0, 0)),
            _resident((D_FF, D_MODEL), lambda i: (0, 0)),
        ],
        out_specs=pl.BlockSpec((TOKEN_TILE, D_MODEL), lambda i: (i, 0)),
        scratch_shapes=[pltpu.VMEM((TOKEN_TILE, D_MODEL), BF16),
                        pltpu.VMEM((TOKEN_TILE, D_MODEL), F32)],
        compiler_params=_params(1),
        name="swiglu_ffn",
    )(x, mod_l, gain, w13, w2)


def _mixer_in_kernel(*refs, rope, emit_state):
    refs = list(refs)
    x_ref, mod_ref, g_ref, win_ref, seg_ref, gq_ref, gk_ref = refs[:7]
    refs = refs[7:]
    if rope:
        cos_ref, sin_first_ref, sin_second_ref = refs[:3]
        refs = refs[3:]
    h_ref, q_ref, k_ref, v_ref, up_ref, uf_ref = refs[:6]
    refs = refs[6:]
    if emit_state:
        sk_ref, sv_ref = refs

    h = _modulated_norm(x_ref[...], g_ref[...], mod_ref, 3).astype(BF16)
    h_ref[...] = h

    def head_norm(t, gain):
        sq = (t * t).astype(BF16)
        width = seg_ref.shape[0]
        mean_sq = jnp.concatenate(
            [_dot(sq[:, width * j:width * (j + 1)], seg_ref[...]) for j in range(QK_WIDTH // width)],
            axis=1)
        return t * lax.rsqrt(mean_sq + EPS) * gain

    def head_chunks(t):
        return [t[:, LANES * i:LANES * (i + 1)] for i in range(N_HEADS)]

    def rotate(chunk):
        return (chunk * cos_ref[...]
                + pltpu.roll(chunk, LANES - HEAD_DIM // 4, axis=1) * sin_first_ref[...]
                + pltpu.roll(chunk, HEAD_DIM // 4, axis=1) * sin_second_ref[...])

    q = head_norm(_dot(h, win_ref[:, 0:QK_WIDTH]), gq_ref[...])
    for i, chunk in enumerate(head_chunks(q)):
        if rope:
            chunk = rotate(chunk)
        q_ref[:, LANES * i:LANES * (i + 1)] = (chunk * (HEAD_DIM ** -0.5 * LOG2_E)).astype(BF16)

    k = head_norm(_dot(h, win_ref[:, QK_WIDTH:2 * QK_WIDTH]), gk_ref[...])
    if emit_state:
        sk_ref[...] = k
    for i, chunk in enumerate(head_chunks(k)):
        if rope:
            chunk = rotate(chunk)
        k_ref[:, LANES * i:LANES * (i + 1)] = chunk.astype(BF16)

    v0 = 2 * QK_WIDTH
    v = _dot(h, win_ref[:, v0:v0 + ATTN_WIDTH])
    if emit_state:
        sv_ref[...] = v
    v_ref[...] = v.astype(BF16)

    p0 = v0 + ATTN_WIDTH
    up_ref[...] = _dot(h, win_ref[:, p0:p0 + POOL_WIDTH])
    f0 = p0 + POOL_WIDTH
    uf_ref[...] = _dot(h, win_ref[:, f0:f0 + FOURIER_WIDTH]).astype(BF16)


def _mixer_in(x, mod_l, mod_row, gain, w_in, seg, gq, gk, rope_tables, *, seq, emit_state):
    n = x.shape[0]
    rope = rope_tables is not None
    tiles_per_seq = seq // TOKEN_TILE
    tok = lambda width: pl.BlockSpec((TOKEN_TILE, width), lambda i: (i, 0))
    in_specs = [
        tok(D_MODEL),
        pl.BlockSpec((None, N_MOD, D_MODEL), lambda i: (mod_row(i), 0, 0)),
        pl.BlockSpec((1, D_MODEL), lambda i: (0, 0)),
        _resident(w_in.shape, lambda i: (0, 0)),
        _resident(seg.shape, lambda i: (0, 0)),
        pl.BlockSpec((1, QK_WIDTH), lambda i: (0, 0)),
        pl.BlockSpec((1, QK_WIDTH), lambda i: (0, 0)),
    ]
    args = [x, mod_l, gain, w_in, seg, gq, gk]
    if rope:
        in_specs += [pl.BlockSpec((TOKEN_TILE, LANES), lambda i: (i % tiles_per_seq, 0))] * 3
        args += list(rope_tables)
    out_shape = [jax.ShapeDtypeStruct((n, D_MODEL), BF16),
                 jax.ShapeDtypeStruct((n, QK_WIDTH), BF16),
                 jax.ShapeDtypeStruct((n, QK_WIDTH), BF16),
                 jax.ShapeDtypeStruct((n, ATTN_WIDTH), BF16),
                 jax.ShapeDtypeStruct((n, POOL_WIDTH), F32),
                 jax.ShapeDtypeStruct((n, FOURIER_WIDTH), BF16)]
    out_specs = [tok(D_MODEL), tok(QK_WIDTH), tok(QK_WIDTH), tok(ATTN_WIDTH),
                 tok(POOL_WIDTH), tok(FOURIER_WIDTH)]
    if emit_state:
        out_shape += [jax.ShapeDtypeStruct((n, QK_WIDTH), F32),
                      jax.ShapeDtypeStruct((n, ATTN_WIDTH), F32)]
        out_specs += [tok(QK_WIDTH), tok(ATTN_WIDTH)]
    return pl.pallas_call(
        functools.partial(_mixer_in_kernel, rope=rope, emit_state=emit_state),
        out_shape=out_shape,
        grid=(n // TOKEN_TILE,),
        in_specs=in_specs,
        out_specs=out_specs,
        compiler_params=_params(1),
        name="mixer_in",
    )(*args)


def _attention_kernel(*refs, n_past, lam_init, heads):
    refs = list(refs)
    q_ref, k_ref, v_ref = refs[:3]
    refs = refs[3:]
    if n_past:
        ck_ref, cv_ref = refs[:2]
        refs = refs[2:]
    lam_ref, gain_ref, o_ref = refs[:3]
    head_lanes = lambda j: slice(LANES * j, LANES * (j + 1))

    if n_past:
        k_all, v_all, k_past, v_past = refs[3:]
        group = pl.program_id(1)
        first_q = pl.program_id(2) == 0

        @pl.when(first_q & (group == 0))
        def _():
            for h in range(N_HEADS):
                k_past[h] = ck_ref[pl.ds(h, n_past, stride=N_HEADS), :].astype(BF16)
                v_past[h] = cv_ref[pl.ds(h, n_past, stride=N_HEADS), :].astype(BF16)

        @pl.when(first_q)
        def _():
            for j in range(heads):
                k_all[j, 0:n_past, :] = k_past[group * heads + j]
                v_all[j, 0:n_past, :] = v_past[group * heads + j]
                k_all[j, n_past:, :] = k_ref[:, head_lanes(j)]
                v_all[j, n_past:, :] = v_ref[:, head_lanes(j)]

        keys_of = lambda j, rows: k_all[j, rows, :]
        values_of = lambda j, rows: v_all[j, rows, :]
    else:
        keys_of = lambda j, rows: k_ref[rows, head_lanes(j)]
        values_of = lambda j, rows: v_ref[rows, head_lanes(j)]

    lq = lam_ref[...]
    lam = (jnp.exp(jnp.sum(lq[0:1, :] * lq[1:2, :], axis=-1, keepdims=True))
           - jnp.exp(jnp.sum(lq[2:3, :] * lq[3:4, :], axis=-1, keepdims=True)) + lam_init)

    n_q = q_ref.shape[0]
    n_keys = n_past + k_ref.shape[0]
    key_chunk = KEY_CHUNK if n_keys % KEY_CHUNK == 0 else n_keys
    lane = lax.broadcasted_iota(jnp.int32, (n_q, LANES), 1)
    zero = jnp.zeros((n_q, LANES), BF16)

    for j in range(heads):
        q = q_ref[:, head_lanes(j)]
        q_both = jnp.concatenate(
            [jnp.where(lane < HEAD_DIM, q, zero), jnp.where(lane >= HEAD_DIM, q, zero)], axis=0)

        run_max = acc = den = None
        for c in range(n_keys // key_chunk):
            rows = slice(key_chunk * c, key_chunk * (c + 1))
            s = _dot_nt(q_both, keys_of(j, rows))
            new_max = jnp.max(s, axis=-1, keepdims=True)
            if c:
                new_max = jnp.maximum(run_max, new_max)
            e = jnp.exp2(s - new_max)
            pv = _dot(e.astype(BF16), values_of(j, rows))
            part = e[:, 0:LANES]
            for t in range(1, key_chunk // LANES):
                part = part + e[:, LANES * t:LANES * (t + 1)]
            if c == 0:
                acc, den = pv, part
            else:
                alpha = jnp.exp2(run_max - new_max)
                acc, den = acc * alpha + pv, den * alpha + part
            run_max = new_max
        o_both = acc / jnp.sum(den, axis=-1, keepdims=True)

        o = o_both[0:n_q] - lam * o_both[n_q:]
        o = o * lax.rsqrt(jnp.mean(o * o, axis=-1, keepdims=True) + EPS) * gain_ref[...]
        o_ref[:, head_lanes(j)] = (o * (1.0 - lam_init)).astype(BF16)


def _attention(q, k, v, cache_k, cache_v, layer, lam_qk, subln_gain, *, lam_init):
    batch, seq, _ = q.shape
    n_past = 0 if cache_k is None else cache_k.shape[2] // N_HEADS
    n_keys = n_past + seq
    q_tile = min(Q_TILE, seq)
    heads = max(1, min(N_HEADS, ATTN_KEYS_PER_STEP // n_keys))
    width = LANES * heads
    seq_block = pl.BlockSpec((None, seq, width), lambda b, g, i: (b, 0, g))
    in_specs = [pl.BlockSpec((None, q_tile, width), lambda b, g, i: (b, i, g)), seq_block, seq_block]
    args = [q, k, v]
    scratch = []
    if n_past:
        past_block = pl.BlockSpec((None, None, n_past * N_HEADS, LANES), lambda b, g, i: (b, layer, 0, 0))
        in_specs += [past_block, past_block]
        args += [cache_k, cache_v]
        scratch = [pltpu.VMEM((heads, n_keys, LANES), BF16), pltpu.VMEM((heads, n_keys, V_DIM), BF16),
                   pltpu.VMEM((N_HEADS, n_past, LANES), BF16), pltpu.VMEM((N_HEADS, n_past, V_DIM), BF16)]
    in_specs += [pl.BlockSpec((4, HEAD_DIM), lambda b, g, i: (0, 0)),
                 pl.BlockSpec((1, V_DIM), lambda b, g, i: (0, 0))]
    args += [lam_qk, subln_gain]
    return pl.pallas_call(
        functools.partial(_attention_kernel, n_past=n_past, lam_init=lam_init, heads=heads),
        out_shape=jax.ShapeDtypeStruct((batch, seq, ATTN_WIDTH), BF16),
        grid=(batch, N_HEADS // heads, seq // q_tile),
        in_specs=in_specs,
        out_specs=pl.BlockSpec((None, q_tile, width), lambda b, g, i: (b, i, g)),
        scratch_shapes=scratch,
        compiler_params=_params(3),
        name="diff_attention",
    )(*args)


def _pool_kernel(u_ref, w_ref, scale_ref, o_ref):
    seq = u_ref.shape[0]
    row = lax.broadcasted_iota(jnp.int32, (seq, POOL_GROUP), 0)
    for g, window in enumerate(POOL_WINDOWS):
        half = window // 2
        lanes = slice(POOL_GROUP * g, POOL_GROUP * (g + 1))
        u = u_ref[:, lanes]
        win_sum = u
        for offset in range(-half, half):
            if offset == 0:
                continue
            shifted = pltpu.roll(u, (-offset) % seq, axis=0)
            src = row + offset
            win_sum = win_sum + jnp.where((src >= 0) & (src < seq), shifted, 0.0)
        count = (jnp.minimum(row + half, seq) - jnp.maximum(row - half, 0)).astype(F32)
        d = win_sum / count - u
        o_ref[:, lanes] = (_dot(d.astype(BF16), w_ref[g]) * scale_ref[:, lanes]).astype(BF16)


def _pool(u, w_pool, scale):
    batch, seq, _ = u.shape
    return pl.pallas_call(
        _pool_kernel,
        out_shape=jax.ShapeDtypeStruct((batch, seq, POOL_WIDTH), BF16),
        grid=(batch,),
        in_specs=[pl.BlockSpec((None, seq, POOL_WIDTH), lambda b: (b, 0, 0)),
                  pl.BlockSpec(w_pool.shape, lambda b: (0, 0, 0)),
                  pl.BlockSpec((1, POOL_WIDTH), lambda b: (0, 0))],
        out_specs=pl.BlockSpec((None, seq, POOL_WIDTH), lambda b: (b, 0, 0)),
        compiler_params=_params(1),
        name="pool_mix",
    )(u, w_pool, scale)


def _fourier_kernel(x_ref, chan_ref, pos_ref, o_ref, t_scr):
    seq = x_ref.shape[0]

    @pl.when(pl.program_id(1) == 0)
    def _():
        for g in range(FOURIER_GROUPS):
            lanes = slice(FOURIER_GROUP * g, FOURIER_GROUP * (g + 1))
            t = _dot(x_ref[:, lanes], chan_ref[...])
            t_scr[0:seq, lanes] = t[:, 0:FOURIER_GROUP].astype(BF16)
            t_scr[seq:2 * seq, lanes] = t[:, FOURIER_GROUP:].astype(BF16)

    scale = 1.0 / math.sqrt(seq * FOURIER_GROUP)
    o_ref[...] = (_dot(pos_ref[...], t_scr[...]) * scale).astype(BF16)


def _fourier(x, pos_table, chan_table):
    batch, seq, _ = x.shape
    row_tile = min(DFT_ROW_TILE, seq)
    return pl.pallas_call(
        _fourier_kernel,
        out_shape=jax.ShapeDtypeStruct((batch, seq, FOURIER_WIDTH), BF16),
        grid=(batch, seq // row_tile),
        in_specs=[pl.BlockSpec((None, seq, FOURIER_WIDTH), lambda b, t: (b, 0, 0)),
                  pl.BlockSpec(chan_table.shape, lambda b, t: (0, 0)),
                  pl.BlockSpec((row_tile, 2 * seq), lambda b, t: (t, 0))],
        out_specs=pl.BlockSpec((None, row_tile, FOURIER_WIDTH), lambda b, t: (b, t, 0)),
        scratch_shapes=[pltpu.VMEM((2 * seq, FOURIER_WIDTH), BF16)],
        compiler_params=_params(2),
        name="fourier_mix",
    )(x, chan_table, pos_table)


def _mixer_out_kernel(x_ref, mod_ref, h_ref, a_ref, p_ref, f_ref,
                      wg_ref, wpa_ref, wpp_ref, wpf_ref, wo_ref, o_ref):
    h = h_ref[...]

    def gated(branch_ref, w_ref, index):
        gate = jax.nn.sigmoid(_dot(h, wg_ref[:, D_MODEL * index:D_MODEL * (index + 1)]))
        return gate * _dot(branch_ref[...], w_ref[...])

    m = gated(a_ref, wpa_ref, 0) + gated(p_ref, wpp_ref, 1) + gated(f_ref, wpf_ref, 2)
    y = _dot(m.astype(BF16), wo_ref[...])
    o_ref[...] = x_ref[...] + mod_ref[5:6, :] * y


def _mixer_out(x, mod_l, mod_row, h, a, p, f, w_gate, w_pa, w_pp, w_pf, w_out):
    n = x.shape[0]
    tok = lambda width: pl.BlockSpec((TOKEN_TILE, width), lambda i: (i, 0))
    weights = [w_gate, w_pa, w_pp, w_pf, w_out]
    return pl.pallas_call(
        _mixer_out_kernel,
        out_shape=jax.ShapeDtypeStruct((n, D_MODEL), F32),
        grid=(n // TOKEN_TILE,),
        in_specs=[tok(D_MODEL),
                  pl.BlockSpec((None, N_MOD, D_MODEL), lambda i: (mod_row(i), 0, 0)),
                  tok(D_MODEL), tok(ATTN_WIDTH), tok(POOL_WIDTH), tok(FOURIER_WIDTH)]
                 + [_resident(w.shape, lambda i: (0, 0)) for w in weights],
        out_specs=tok(D_MODEL),
        compiler_params=_params(1),
        name="mixer_out",
    )(x, mod_l, h, a, p, f, *weights)


def _trunk_layer(x, batch, seq, mod_l, mod_row, cache, layer, lam_init, w, tables):
    x = _ffn(x, mod_l, mod_row, w["norm_g"][0], *w["ffn"][0], mod_first=0)

    emit_state = cache is None
    outs = _mixer_in(x, mod_l, mod_row, w["norm_g"][1], w["w_in"], tables["seg"], w["gq"], w["gk"],
                     None if emit_state else tables["rope"], seq=seq, emit_state=emit_state)
    h, q, k, v, up, uf = outs[:6]
    per_seq = lambda t: t.reshape(batch, seq, t.shape[-1])
    cache_k, cache_v = (None, None) if emit_state else cache
    a = _attention(per_seq(q), per_seq(k), per_seq(v), cache_k, cache_v, layer,
                   w["lam_qk"], w["subln_g"], lam_init=lam_init)
    p = _pool(per_seq(up), w["w_pool"], w["pool_scale"])
    f = _fourier(per_seq(uf), *tables["dft"])
    flat = lambda t: t.reshape(batch * seq, t.shape[-1])
    x = _mixer_out(x, mod_l, mod_row, h, flat(a), flat(p), flat(f),
                   w["w_gate"], w["w_pa"], w["w_pp"], w["w_pf"], w["w_out"])

    x = _ffn(x, mod_l, mod_row, w["norm_g"][2], *w["ffn"][1], mod_first=6)
    state = (outs[6], outs[7]) if emit_state else (None, None)
    return x, state


def kernel(x_prompt, x_sample, cache_k, cache_v, c, c_ctx, w_ada, b_ada, norm_g, ffn_w13, ffn_w2, w_in, q_norm_g, k_norm_g, lam_qk, subln_g, w_pool, pool_scale, w_gate, w_pa, w_pp, w_pf, w_out):
    batch, seq, _ = x_prompt.shape
    dec_batch, dec_seq, _ = x_sample.shape
    depth = w_ada.shape[0]
    n_past = cache_k.shape[2]
    assert dec_batch + 1 <= MOD_ROWS
    assert seq % TOKEN_TILE == 0 or TOKEN_TILE % seq == 0
    assert dec_seq % TOKEN_TILE == 0 and (batch * seq) % TOKEN_TILE == 0

    cond = jnp.concatenate(
        [c, c_ctx[None], jnp.zeros((MOD_ROWS - dec_batch - 1, D_MODEL), F32)], axis=0)
    mod = _modulation(cond, w_ada, b_ada)

    cast = lambda t: t.astype(BF16)
    w13, w2, w_in_b, w_gate_b, w_pa_b, w_pp_b, w_pf_b, w_out_b, w_pool_b = map(
        cast, (ffn_w13, ffn_w2, w_in, w_gate, w_pa, w_pp, w_pf, w_out, w_pool))
    tile_gain = lambda g: jnp.tile(g, QK_WIDTH // HEAD_DIM)[None, :]

    seg = jnp.asarray(_segment_mean_matrix()).astype(BF16)
    tables_ctx = {"seg": seg, "rope": None,
                  "dft": tuple(jnp.asarray(t).astype(BF16) for t in _dft_tables(seq))}
    tables_lat = {"seg": seg, "rope": tuple(jnp.asarray(t) for t in _rope_tables(dec_seq)),
                  "dft": tuple(jnp.asarray(t).astype(BF16) for t in _dft_tables(dec_seq))}
    cache = (cache_k.reshape(dec_batch, depth, n_past * N_HEADS, 2 * HEAD_DIM),
             cache_v.reshape(dec_batch, depth, n_past * N_HEADS, V_DIM))

    ctx_row = lambda i: dec_batch
    lat_tiles = dec_seq // TOKEN_TILE
    lat_row = lambda i: i // lat_tiles

    xp = x_prompt.reshape(batch * seq, D_MODEL)
    xs = x_sample.reshape(dec_batch * dec_seq, D_MODEL)
    new_k, new_v = [], []
    for l in range(depth):
        lam_init = 0.8 - 0.6 * math.exp(-0.3 * l)
        w = {
            "norm_g": [norm_g[l, s][None, :] for s in range(3)],
            "ffn": [(w13[l, s], w2[l, s]) for s in range(2)],
            "w_in": w_in_b[l], "gq": tile_gain(q_norm_g[l]), "gk": tile_gain(k_norm_g[l]),
            "lam_qk": lam_qk[l], "subln_g": subln_g[l][None, :],
            "w_pool": w_pool_b[l], "pool_scale": pool_scale[l][None, :],
            "w_gate": w_gate_b[l], "w_pa": w_pa_b[l], "w_pp": w_pp_b[l], "w_pf": w_pf_b[l],
            "w_out": w_out_b[l],
        }
        xp, (k_l, v_l) = _trunk_layer(xp, batch, seq, mod[l], ctx_row, None, l, lam_init, w, tables_ctx)
        new_k.append(k_l.reshape(batch, seq, N_HEADS, 2 * HEAD_DIM))
        new_v.append(v_l.reshape(batch, seq, N_HEADS, V_DIM))
        xs, _ = _trunk_layer(xs, dec_batch, dec_seq, mod[l], lat_row, cache, l, lam_init, w, tables_lat)

    return (xp.reshape(batch, seq, D_MODEL), xs.reshape(dec_batch, dec_seq, D_MODEL),
            jnp.stack(new_k, axis=1), jnp.stack(new_v, axis=1))
```

```python
import functools
import math

import numpy as np
import jax
import jax.numpy as jnp
from jax import lax
from jax.experimental import pallas as pl
from jax.experimental.pallas import tpu as pltpu

D_MODEL = 1024
N_HEADS = 8
HEAD_DIM = 64
V_DIM = 2 * HEAD_DIM
QK_WIDTH = N_HEADS * 2 * HEAD_DIM
ATTN_WIDTH = N_HEADS * V_DIM
POOL_WINDOWS = (2, 4, 8, 16)
POOL_GROUP = 128
POOL_WIDTH = POOL_GROUP * len(POOL_WINDOWS)
FOURIER_GROUPS = 4
FOURIER_GROUP = 128
FOURIER_WIDTH = FOURIER_GROUPS * FOURIER_GROUP
D_FF = 2816
N_MOD = 9
GRID_W = 64
ROPE_BASE = 10000.0
EPS = 1e-6

LANES = 128
MXU_WIDTH = 256
VMEM_LIMIT_BYTES = 56 * 1024 * 1024

MOD_ROWS = 16
MOD_COL_TILE = 1152
FF_CHUNK = 256
TOKEN_TILE = 512
FFN_TOKEN_TILE = 1024
Q_TILE = 1024
KEY_CHUNK = 256
ATTN_KEYS_PER_STEP = 4096
LOG2_E = math.log2(math.e)
DFT_ROW_TILE = 512

F32 = jnp.float32
BF16 = jnp.bfloat16


def _dot(a, b):
    return jnp.dot(a, b, preferred_element_type=F32)


def _dot_nt(a, b):
    return lax.dot_general(a, b, (((1,), (1,)), ((), ())), preferred_element_type=F32)


def _resident(block_shape, index_map):
    return pl.BlockSpec(block_shape, index_map, pipeline_mode=pl.Buffered(1))


def _per_layer(array, *lead, resident=False):
    tail = array.shape[len(lead):]
    index = tuple(lead) + (0,) * len(tail)
    make = _resident if resident else pl.BlockSpec
    return make((None,) * len(lead) + tail, lambda *_: index)


def _mod_spec(layer, mod_row):
    return pl.BlockSpec((None, None, N_MOD, D_MODEL), lambda i: (layer, mod_row(i), 0, 0))


def _params(n_axes):
    return pltpu.CompilerParams(
        dimension_semantics=("arbitrary",) * n_axes, vmem_limit_bytes=VMEM_LIMIT_BYTES)


def _modulated_norm(x, gain, mod_ref, first):
    y = x * lax.rsqrt(jnp.mean(x * x, axis=-1, keepdims=True) + EPS) * gain
    return y * (1.0 + mod_ref[first + 1:first + 2, :]) + mod_ref[first:first + 1, :]


@functools.lru_cache(maxsize=None)
def _rope_tables(seq):
    n_freq = HEAD_DIM // 4
    lane = np.arange(LANES)
    within = lane % HEAD_DIM
    axis = within // (2 * n_freq)
    half = (within % (2 * n_freq)) // n_freq
    freq = lane % n_freq
    inv = ROPE_BASE ** (-np.arange(n_freq, dtype=np.float64) / n_freq)
    pos = np.arange(seq)
    coord = np.where(axis[None, :] == 0, (pos // GRID_W)[:, None], (pos % GRID_W)[:, None])
    ang = coord.astype(np.float64) * inv[freq][None, :]
    cos, sin = np.cos(ang), np.sin(ang)
    sin_first = np.where(half[None, :] == 0, -sin, 0.0)
    sin_second = np.where(half[None, :] == 1, sin, 0.0)
    return tuple(np.asarray(t, np.float32) for t in (cos, sin_first, sin_second))


@functools.lru_cache(maxsize=None)
def _segment_mean_matrix():
    idx = np.arange(MXU_WIDTH) // HEAD_DIM
    return np.asarray((idx[:, None] == idx[None, :]) / HEAD_DIM, np.float32)


@functools.lru_cache(maxsize=None)
def _dft_tables(seq):
    def cs(n):
        k = np.arange(n, dtype=np.int64)
        ang = 2.0 * np.pi * ((k[:, None] * k[None, :]) % n).astype(np.float64) / n
        return np.cos(ang), np.sin(ang)
    cl, sl = cs(seq)
    cc, sc = cs(FOURIER_GROUP)
    return (np.asarray(np.concatenate([cl, -sl], axis=1), np.float32),
            np.asarray(np.concatenate([cc, sc], axis=1), np.float32))


def _mod_kernel(c_ref, w_ref, b_ref, o_ref):
    c = c_ref[...]
    s = c * jax.nn.sigmoid(c)
    s_hi = s.astype(BF16)
    s_lo = (s - s_hi.astype(F32)).astype(BF16)
    w = w_ref[...]
    w_hi = w.astype(BF16)
    w_lo = (w - w_hi.astype(F32)).astype(BF16)
    o_ref[...] = _dot(s_hi, w_hi) + _dot(s_lo, w_hi) + _dot(s_hi, w_lo) + b_ref[...]


def _modulation(cond, w_ada, b_ada):
    depth = w_ada.shape[0]
    width = N_MOD * D_MODEL
    out = pl.pallas_call(
        _mod_kernel,
        out_shape=jax.ShapeDtypeStruct((depth, MOD_ROWS, width), F32),
        grid=(depth, width // MOD_COL_TILE),
        in_specs=[
            pl.BlockSpec((MOD_ROWS, D_MODEL), lambda l, n: (0, 0)),
            pl.BlockSpec((None, D_MODEL, MOD_COL_TILE), lambda l, n: (l, 0, n)),
            pl.BlockSpec((None, 1, MOD_COL_TILE), lambda l, n: (l, 0, n)),
        ],
        out_specs=pl.BlockSpec((None, MOD_ROWS, MOD_COL_TILE), lambda l, n: (l, 0, n)),
        compiler_params=_params(2),
        name="adaln_modulation",
    )(cond, w_ada, b_ada.reshape(depth, 1, width))
    return out.reshape(depth, MOD_ROWS, N_MOD, D_MODEL)


def _ffn_kernel(x_ref, mod_ref, g_ref, w13_ref, w2_ref, o_ref, h_scr, acc_scr, *, mod_first):
    x = x_ref[...]
    h_scr[...] = _modulated_norm(x, g_ref[...], mod_ref, mod_first).astype(BF16)
    acc_scr[...] = jnp.zeros_like(acc_scr)

    for j in range(D_FF // FF_CHUNK):
        cols = slice(FF_CHUNK * j, FF_CHUNK * (j + 1))
        h = h_scr[...]
        gate = _dot(h, w13_ref[:, cols])
        up = _dot(h, w13_ref[:, D_FF + FF_CHUNK * j:D_FF + FF_CHUNK * (j + 1)])
        act = (gate * jax.nn.sigmoid(gate)) * up
        acc_scr[...] += _dot(act.astype(BF16), w2_ref[cols, :])

    o_ref[...] = x + (0.5 * mod_ref[mod_first + 2:mod_first + 3, :]) * acc_scr[...]


def _ffn(x, p, layer, half, mod_row):
    n = x.shape[0]
    tile = FFN_TOKEN_TILE
    return pl.pallas_call(
        functools.partial(_ffn_kernel, mod_first=6 * half),
        out_shape=jax.ShapeDtypeStruct((n, D_MODEL), F32),
        grid=(n // tile,),
        in_specs=[
            pl.BlockSpec((tile, D_MODEL), lambda i: (i, 0)),
            _mod_spec(layer, mod_row(tile)),
            _per_layer(p["norm_g"], layer, 2 * half),
            _per_layer(p["w13"], layer, half, resident=True),
            _per_layer(p["w2"], layer, half, resident=True),
        ],
        out_specs=pl.BlockSpec((tile, D_MODEL), lambda i: (i, 0)),
        scratch_shapes=[pltpu.VMEM((tile, D_MODEL), BF16), pltpu.VMEM((tile, D_MODEL), F32)],
        compiler_params=_params(1),
        name="swiglu_ffn",
    )(x, p["mod"], p["norm_g"], p["w13"], p["w2"])


def _mixer_in_kernel(*refs, rope, emit_state):
    refs = list(refs)
    x_ref, mod_ref, g_ref, win_ref, seg_ref, gq_ref, gk_ref = refs[:7]
    refs = refs[7:]
    if rope:
        cos_ref, sin_first_ref, sin_second_ref = refs[:3]
        refs = refs[3:]
    h_ref, q_ref, k_ref, v_ref, up_ref, uf_ref = refs[:6]
    refs = refs[6:]
    if emit_state:
        sk_ref, sv_ref = refs

    h = _modulated_norm(x_ref[...], g_ref[...], mod_ref, 3).astype(BF16)
    h_ref[...] = h

    def head_norm(t, gain):
        sq = (t * t).astype(BF16)
        width = seg_ref.shape[0]
        mean_sq = jnp.concatenate(
            [_dot(sq[:, width * j:width * (j + 1)], seg_ref[...]) for j in range(QK_WIDTH // width)],
            axis=1)
        return t * lax.rsqrt(mean_sq + EPS) * gain

    def head_chunks(t):
        return [t[:, LANES * i:LANES * (i + 1)] for i in range(N_HEADS)]

    def rotate(chunk):
        return (chunk * cos_ref[...]
                + pltpu.roll(chunk, LANES - HEAD_DIM // 4, axis=1) * sin_first_ref[...]
                + pltpu.roll(chunk, HEAD_DIM // 4, axis=1) * sin_second_ref[...])

    q = head_norm(_dot(h, win_ref[:, 0:QK_WIDTH]), gq_ref[...])
    for i, chunk in enumerate(head_chunks(q)):
        if rope:
            chunk = rotate(chunk)
        q_ref[:, LANES * i:LANES * (i + 1)] = (chunk * (HEAD_DIM ** -0.5 * LOG2_E)).astype(BF16)

    k = head_norm(_dot(h, win_ref[:, QK_WIDTH:2 * QK_WIDTH]), gk_ref[...])
    if emit_state:
        sk_ref[...] = k
    for i, chunk in enumerate(head_chunks(k)):
        if rope:
            chunk = rotate(chunk)
        k_ref[:, LANES * i:LANES * (i + 1)] = chunk.astype(BF16)

    v0 = 2 * QK_WIDTH
    v = _dot(h, win_ref[:, v0:v0 + ATTN_WIDTH])
    if emit_state:
        sv_ref[...] = v
    v_ref[...] = v.astype(BF16)

    p0 = v0 + ATTN_WIDTH
    up_ref[...] = _dot(h, win_ref[:, p0:p0 + POOL_WIDTH])
    f0 = p0 + POOL_WIDTH
    uf_ref[...] = _dot(h, win_ref[:, f0:f0 + FOURIER_WIDTH]).astype(BF16)


def _mixer_in(x, p, layer, mod_row, rope_tables, *, seq, emit_state):
    n = x.shape[0]
    rope = rope_tables is not None
    tiles_per_seq = seq // TOKEN_TILE
    tok = lambda width: pl.BlockSpec((TOKEN_TILE, width), lambda i: (i, 0))
    in_specs = [
        tok(D_MODEL),
        _mod_spec(layer, mod_row(TOKEN_TILE)),
        _per_layer(p["norm_g"], layer, 1),
        _per_layer(p["w_in"], layer, resident=True),
        _resident(p["seg"].shape, lambda i: (0, 0)),
        _per_layer(p["q_gain"], layer),
        _per_layer(p["k_gain"], layer),
    ]
    args = [x, p["mod"], p["norm_g"], p["w_in"], p["seg"], p["q_gain"], p["k_gain"]]
    if rope:
        in_specs += [pl.BlockSpec((TOKEN_TILE, LANES), lambda i: (i % tiles_per_seq, 0))] * 3
        args += list(rope_tables)
    out_shape = [jax.ShapeDtypeStruct((n, D_MODEL), BF16),
                 jax.ShapeDtypeStruct((n, QK_WIDTH), BF16),
                 jax.ShapeDtypeStruct((n, QK_WIDTH), BF16),
                 jax.ShapeDtypeStruct((n, ATTN_WIDTH), BF16),
                 jax.ShapeDtypeStruct((n, POOL_WIDTH), F32),
                 jax.ShapeDtypeStruct((n, FOURIER_WIDTH), BF16)]
    out_specs = [tok(D_MODEL), tok(QK_WIDTH), tok(QK_WIDTH), tok(ATTN_WIDTH),
                 tok(POOL_WIDTH), tok(FOURIER_WIDTH)]
    if emit_state:
        out_shape += [jax.ShapeDtypeStruct((n, QK_WIDTH), F32),
                      jax.ShapeDtypeStruct((n, ATTN_WIDTH), F32)]
        out_specs += [tok(QK_WIDTH), tok(ATTN_WIDTH)]
    return pl.pallas_call(
        functools.partial(_mixer_in_kernel, rope=rope, emit_state=emit_state),
        out_shape=out_shape,
        grid=(n // TOKEN_TILE,),
        in_specs=in_specs,
        out_specs=out_specs,
        compiler_params=_params(1),
        name="mixer_in",
    )(*args)


def _attention_kernel(*refs, n_past, lam_init, heads):
    refs = list(refs)
    q_ref, k_ref, v_ref = refs[:3]
    refs = refs[3:]
    if n_past:
        ck_ref, cv_ref = refs[:2]
        refs = refs[2:]
    lam_ref, gain_ref, o_ref = refs[:3]
    head_lanes = lambda j: slice(LANES * j, LANES * (j + 1))

    if n_past:
        k_all, v_all, k_past, v_past = refs[3:]
        group = pl.program_id(1)
        first_q = pl.program_id(2) == 0

        @pl.when(first_q & (group == 0))
        def _():
            for h in range(N_HEADS):
                k_past[h] = ck_ref[pl.ds(h, n_past, stride=N_HEADS), :].astype(BF16)
                v_past[h] = cv_ref[pl.ds(h, n_past, stride=N_HEADS), :].astype(BF16)

        @pl.when(first_q)
        def _():
            for j in range(heads):
                k_all[j, 0:n_past, :] = k_past[group * heads + j]
                v_all[j, 0:n_past, :] = v_past[group * heads + j]
                k_all[j, n_past:, :] = k_ref[:, head_lanes(j)]
                v_all[j, n_past:, :] = v_ref[:, head_lanes(j)]

        keys_of = lambda j, rows: k_all[j, rows, :]
        values_of = lambda j, rows: v_all[j, rows, :]
    else:
        keys_of = lambda j, rows: k_ref[rows, head_lanes(j)]
        values_of = lambda j, rows: v_ref[rows, head_lanes(j)]

    lq = lam_ref[...]
    lam = (jnp.exp(jnp.sum(lq[0:1, :] * lq[1:2, :], axis=-1, keepdims=True))
           - jnp.exp(jnp.sum(lq[2:3, :] * lq[3:4, :], axis=-1, keepdims=True)) + lam_init)

    n_q = q_ref.shape[0]
    n_keys = n_past + k_ref.shape[0]
    key_chunk = KEY_CHUNK if n_keys % KEY_CHUNK == 0 else n_keys
    lane = lax.broadcasted_iota(jnp.int32, (n_q, LANES), 1)
    zero = jnp.zeros((n_q, LANES), BF16)

    for j in range(heads):
        q = q_ref[:, head_lanes(j)]
        q_both = jnp.concatenate(
            [jnp.where(lane < HEAD_DIM, q, zero), jnp.where(lane >= HEAD_DIM, q, zero)], axis=0)

        run_max = acc = den = None
        for c in range(n_keys // key_chunk):
            rows = slice(key_chunk * c, key_chunk * (c + 1))
            s = _dot_nt(q_both, keys_of(j, rows))
            new_max = jnp.max(s, axis=-1, keepdims=True)
            if c:
                new_max = jnp.maximum(run_max, new_max)
            e = jnp.exp2(s - new_max)
            pv = _dot(e.astype(BF16), values_of(j, rows))
            part = e[:, 0:LANES]
            for t in range(1, key_chunk // LANES):
                part = part + e[:, LANES * t:LANES * (t + 1)]
            if c == 0:
                acc, den = pv, part
            else:
                alpha = jnp.exp2(run_max - new_max)
                acc, den = acc * alpha + pv, den * alpha + part
            run_max = new_max
        o_both = acc / jnp.sum(den, axis=-1, keepdims=True)

        o = o_both[0:n_q] - lam * o_both[n_q:]
        o = o * lax.rsqrt(jnp.mean(o * o, axis=-1, keepdims=True) + EPS) * gain_ref[...]
        o_ref[:, head_lanes(j)] = (o * (1.0 - lam_init)).astype(BF16)


def _attention(q, k, v, cache_k, cache_v, p, layer):
    batch, seq, _ = q.shape
    lam_init = 0.8 - 0.6 * math.exp(-0.3 * layer)
    n_past = 0 if cache_k is None else cache_k.shape[2] // N_HEADS
    n_keys = n_past + seq
    q_tile = min(Q_TILE, seq)
    heads = max(1, min(N_HEADS, ATTN_KEYS_PER_STEP // n_keys))
    width = LANES * heads
    seq_block = pl.BlockSpec((None, seq, width), lambda b, g, i: (b, 0, g))
    in_specs = [pl.BlockSpec((None, q_tile, width), lambda b, g, i: (b, i, g)), seq_block, seq_block]
    args = [q, k, v]
    scratch = []
    if n_past:
        past_block = pl.BlockSpec((None, None, n_past * N_HEADS, LANES), lambda b, g, i: (b, layer, 0, 0))
        in_specs += [past_block, past_block]
        args += [cache_k, cache_v]
        scratch = [pltpu.VMEM((heads, n_keys, LANES), BF16), pltpu.VMEM((heads, n_keys, V_DIM), BF16),
                   pltpu.VMEM((N_HEADS, n_past, LANES), BF16), pltpu.VMEM((N_HEADS, n_past, V_DIM), BF16)]
    in_specs += [_per_layer(p["lam_qk"], layer), _per_layer(p["subln_g"], layer)]
    args += [p["lam_qk"], p["subln_g"]]
    return pl.pallas_call(
        functools.partial(_attention_kernel, n_past=n_past, lam_init=lam_init, heads=heads),
        out_shape=jax.ShapeDtypeStruct((batch, seq, ATTN_WIDTH), BF16),
        grid=(batch, N_HEADS // heads, seq // q_tile),
        in_specs=in_specs,
        out_specs=pl.BlockSpec((None, q_tile, width), lambda b, g, i: (b, i, g)),
        scratch_shapes=scratch,
        compiler_params=_params(3),
        name="diff_attention",
    )(*args)


def _pool_kernel(u_ref, w_ref, scale_ref, o_ref):
    seq = u_ref.shape[0]
    row = lax.broadcasted_iota(jnp.int32, (seq, POOL_GROUP), 0)
    for g, window in enumerate(POOL_WINDOWS):
        half = window // 2
        lanes = slice(POOL_GROUP * g, POOL_GROUP * (g + 1))
        u = u_ref[:, lanes]
        win_sum = u
        for offset in range(-half, half):
            if offset == 0:
                continue
            shifted = pltpu.roll(u, (-offset) % seq, axis=0)
            src = row + offset
            win_sum = win_sum + jnp.where((src >= 0) & (src < seq), shifted, 0.0)
        count = (jnp.minimum(row + half, seq) - jnp.maximum(row - half, 0)).astype(F32)
        d = win_sum / count - u
        o_ref[:, lanes] = (_dot(d.astype(BF16), w_ref[g]) * scale_ref[:, lanes]).astype(BF16)


def _pool(u, p, layer):
    batch, seq, _ = u.shape
    return pl.pallas_call(
        _pool_kernel,
        out_shape=jax.ShapeDtypeStruct((batch, seq, POOL_WIDTH), BF16),
        grid=(batch,),
        in_specs=[pl.BlockSpec((None, seq, POOL_WIDTH), lambda b: (b, 0, 0)),
                  _per_layer(p["w_pool"], layer),
                  _per_layer(p["pool_scale"], layer)],
        out_specs=pl.BlockSpec((None, seq, POOL_WIDTH), lambda b: (b, 0, 0)),
        compiler_params=_params(1),
        name="pool_mix",
    )(u, p["w_pool"], p["pool_scale"])


def _fourier_kernel(x_ref, chan_ref, pos_ref, o_ref, t_scr):
    seq = x_ref.shape[0]

    @pl.when(pl.program_id(1) == 0)
    def _():
        for g in range(FOURIER_GROUPS):
            lanes = slice(FOURIER_GROUP * g, FOURIER_GROUP * (g + 1))
            t = _dot(x_ref[:, lanes], chan_ref[...])
            t_scr[0:seq, lanes] = t[:, 0:FOURIER_GROUP].astype(BF16)
            t_scr[seq:2 * seq, lanes] = t[:, FOURIER_GROUP:].astype(BF16)

    scale = 1.0 / math.sqrt(seq * FOURIER_GROUP)
    o_ref[...] = (_dot(pos_ref[...], t_scr[...]) * scale).astype(BF16)


def _fourier(x, pos_table, chan_table):
    batch, seq, _ = x.shape
    row_tile = min(DFT_ROW_TILE, seq)
    return pl.pallas_call(
        _fourier_kernel,
        out_shape=jax.ShapeDtypeStruct((batch, seq, FOURIER_WIDTH), BF16),
        grid=(batch, seq // row_tile),
        in_specs=[pl.BlockSpec((None, seq, FOURIER_WIDTH), lambda b, t: (b, 0, 0)),
                  pl.BlockSpec(chan_table.shape, lambda b, t: (0, 0)),
                  pl.BlockSpec((row_tile, 2 * seq), lambda b, t: (t, 0))],
        out_specs=pl.BlockSpec((None, row_tile, FOURIER_WIDTH), lambda b, t: (b, t, 0)),
        scratch_shapes=[pltpu.VMEM((2 * seq, FOURIER_WIDTH), BF16)],
        compiler_params=_params(2),
        name="fourier_mix",
    )(x, chan_table, pos_table)


def _mixer_out_kernel(x_ref, mod_ref, h_ref, a_ref, p_ref, f_ref,
                      wg_ref, wpa_ref, wpp_ref, wpf_ref, wo_ref, o_ref):
    h = h_ref[...]

    def gated(branch_ref, w_ref, index):
        gate = jax.nn.sigmoid(_dot(h, wg_ref[:, D_MODEL * index:D_MODEL * (index + 1)]))
        return gate * _dot(branch_ref[...], w_ref[...])

    m = gated(a_ref, wpa_ref, 0) + gated(p_ref, wpp_ref, 1) + gated(f_ref, wpf_ref, 2)
    y = _dot(m.astype(BF16), wo_ref[...])
    o_ref[...] = x_ref[...] + mod_ref[5:6, :] * y


def _mixer_out(x, p, layer, mod_row, h, attn, pooled, fourier):
    n = x.shape[0]
    tok = lambda width: pl.BlockSpec((TOKEN_TILE, width), lambda i: (i, 0))
    names = ["w_gate", "w_pa", "w_pp", "w_pf", "w_out"]
    return pl.pallas_call(
        _mixer_out_kernel,
        out_shape=jax.ShapeDtypeStruct((n, D_MODEL), F32),
        grid=(n // TOKEN_TILE,),
        in_specs=[tok(D_MODEL), _mod_spec(layer, mod_row(TOKEN_TILE)),
                  tok(D_MODEL), tok(ATTN_WIDTH), tok(POOL_WIDTH), tok(FOURIER_WIDTH)]
                 + [_per_layer(p[name], layer, resident=True) for name in names],
        out_specs=tok(D_MODEL),
        compiler_params=_params(1),
        name="mixer_out",
    )(x, p["mod"], h, attn, pooled, fourier, *[p[name] for name in names])


def _trunk_layer(x, batch, seq, p, layer, mod_row, cache, rope_tables, dft_tables):
    x = _ffn(x, p, layer, 0, mod_row)

    emit_state = cache is None
    outs = _mixer_in(x, p, layer, mod_row, rope_tables, seq=seq, emit_state=emit_state)
    h, q, k, v, up, uf = outs[:6]
    per_seq = lambda t: t.reshape(batch, seq, t.shape[-1])
    cache_k, cache_v = (None, None) if emit_state else cache
    attn = _attention(per_seq(q), per_seq(k), per_seq(v), cache_k, cache_v, p, layer)
    pooled = _pool(per_seq(up), p, layer)
    fourier = _fourier(per_seq(uf), *dft_tables)
    flat = lambda t: t.reshape(batch * seq, t.shape[-1])
    x = _mixer_out(x, p, layer, mod_row, h, flat(attn), flat(pooled), flat(fourier))

    x = _ffn(x, p, layer, 1, mod_row)
    state = (outs[6], outs[7]) if emit_state else (None, None)
    return x, state


def kernel(x_prompt, x_sample, cache_k, cache_v, c, c_ctx, w_ada, b_ada, norm_g, ffn_w13, ffn_w2, w_in, q_norm_g, k_norm_g, lam_qk, subln_g, w_pool, pool_scale, w_gate, w_pa, w_pp, w_pf, w_out):
    batch, seq, _ = x_prompt.shape
    dec_batch, dec_seq, _ = x_sample.shape
    depth = w_ada.shape[0]
    n_past = cache_k.shape[2]
    assert dec_batch + 1 <= MOD_ROWS
    for tile in (TOKEN_TILE, FFN_TOKEN_TILE):
        assert dec_seq % tile == 0 and (batch * seq) % tile == 0

    cond = jnp.concatenate(
        [c, c_ctx[None], jnp.zeros((MOD_ROWS - dec_batch - 1, D_MODEL), F32)], axis=0)

    cast = lambda t: t.astype(BF16)
    rows = lambda t: t.reshape(t.shape[:-1] + (1, t.shape[-1]))
    tile_gain = lambda g: rows(jnp.tile(g, (1, QK_WIDTH // HEAD_DIM)))
    p = {
        "mod": _modulation(cond, w_ada, b_ada),
        "norm_g": rows(norm_g), "w13": cast(ffn_w13), "w2": cast(ffn_w2), "w_in": cast(w_in),
        "seg": cast(jnp.asarray(_segment_mean_matrix())),
        "q_gain": tile_gain(q_norm_g), "k_gain": tile_gain(k_norm_g),
        "lam_qk": lam_qk, "subln_g": rows(subln_g),
        "w_pool": cast(w_pool), "pool_scale": rows(pool_scale),
        "w_gate": cast(w_gate), "w_pa": cast(w_pa), "w_pp": cast(w_pp), "w_pf": cast(w_pf),
        "w_out": cast(w_out),
    }
    dft_ctx = tuple(cast(jnp.asarray(t)) for t in _dft_tables(seq))
    dft_lat = tuple(cast(jnp.asarray(t)) for t in _dft_tables(dec_seq))
    rope_lat = tuple(jnp.asarray(t) for t in _rope_tables(dec_seq))
    cache = (cache_k.reshape(dec_batch, depth, n_past * N_HEADS, 2 * HEAD_DIM),
             cache_v.reshape(dec_batch, depth, n_past * N_HEADS, V_DIM))

    ctx_row = lambda tile: (lambda i: dec_batch)
    lat_row = lambda tile: (lambda i: (i * tile) // dec_seq)

    xp = x_prompt.reshape(batch * seq, D_MODEL)
    xs = x_sample.reshape(dec_batch * dec_seq, D_MODEL)
    new_k, new_v = [], []
    for l in range(depth):
        xp, (k_l, v_l) = _trunk_layer(xp, batch, seq, p, l, ctx_row, None, None, dft_ctx)
        new_k.append(k_l.reshape(batch, seq, N_HEADS, 2 * HEAD_DIM))
        new_v.append(v_l.reshape(batch, seq, N_HEADS, V_DIM))
        xs, _ = _trunk_layer(xs, dec_batch, dec_seq, p, l, lat_row, cache, rope_lat, dft_lat)

    return (xp.reshape(batch, seq, D_MODEL), xs.reshape(dec_batch, dec_seq, D_MODEL),
            jnp.stack(new_k, axis=1), jnp.stack(new_v, axis=1))
```

```python
import functools
import math

import numpy as np
import jax
import jax.numpy as jnp
from jax import lax
from jax.experimental import pallas as pl
from jax.experimental.pallas import tpu as pltpu

D_MODEL = 1024
N_HEADS = 8
HEAD_DIM = 64
V_DIM = 2 * HEAD_DIM
QK_WIDTH = N_HEADS * 2 * HEAD_DIM
ATTN_WIDTH = N_HEADS * V_DIM
POOL_WINDOWS = (2, 4, 8, 16)
POOL_GROUP = 128
POOL_WIDTH = POOL_GROUP * len(POOL_WINDOWS)
FOURIER_GROUPS = 4
FOURIER_GROUP = 128
FOURIER_WIDTH = FOURIER_GROUPS * FOURIER_GROUP
D_FF = 2816
N_MOD = 9
GRID_W = 64
ROPE_BASE = 10000.0
EPS = 1e-6

LANES = 128
MXU_WIDTH = 256
VMEM_LIMIT_BYTES = 56 * 1024 * 1024

MOD_ROWS = 16
MOD_COL_TILE = 1152
FF_CHUNK = 256
TOKEN_TILE = 512
FFN_TOKEN_TILE = 1024
Q_TILE = 2048
KEY_CHUNK = 256
ATTN_KEYS_PER_STEP = 4096
LOG2_E = math.log2(math.e)
DFT_BLOCK = MXU_WIDTH

F32 = jnp.float32
BF16 = jnp.bfloat16


def _dot(a, b):
    return jnp.dot(a, b, preferred_element_type=F32)


def _dot_nt(a, b):
    return lax.dot_general(a, b, (((1,), (1,)), ((), ())), preferred_element_type=F32)


def _resident(block_shape, index_map):
    return pl.BlockSpec(block_shape, index_map, pipeline_mode=pl.Buffered(1))


def _per_layer(array, *lead, resident=False):
    tail = array.shape[len(lead):]
    index = tuple(lead) + (0,) * len(tail)
    make = _resident if resident else pl.BlockSpec
    return make((None,) * len(lead) + tail, lambda *_: index)


def _mod_spec(layer, mod_row):
    return pl.BlockSpec((None, None, N_MOD, D_MODEL), lambda i: (layer, mod_row(i), 0, 0))


def _params(n_axes):
    return pltpu.CompilerParams(
        dimension_semantics=("arbitrary",) * n_axes, vmem_limit_bytes=VMEM_LIMIT_BYTES)


def _modulated_norm(x, gain, mod_ref, first):
    y = x * lax.rsqrt(jnp.mean(x * x, axis=-1, keepdims=True) + EPS) * gain
    return y * (1.0 + mod_ref[first + 1:first + 2, :]) + mod_ref[first:first + 1, :]


@functools.lru_cache(maxsize=None)
def _rope_tables(seq):
    n_freq = HEAD_DIM // 4
    lane = np.arange(LANES)
    within = lane % HEAD_DIM
    axis = within // (2 * n_freq)
    half = (within % (2 * n_freq)) // n_freq
    freq = lane % n_freq
    inv = ROPE_BASE ** (-np.arange(n_freq, dtype=np.float64) / n_freq)
    pos = np.arange(seq)
    coord = np.where(axis[None, :] == 0, (pos // GRID_W)[:, None], (pos % GRID_W)[:, None])
    ang = coord.astype(np.float64) * inv[freq][None, :]
    cos, sin = np.cos(ang), np.sin(ang)
    sin_first = np.where(half[None, :] == 0, -sin, 0.0)
    sin_second = np.where(half[None, :] == 1, sin, 0.0)
    return tuple(np.asarray(t, np.float32) for t in (cos, sin_first, sin_second))


@functools.lru_cache(maxsize=None)
def _segment_mean_matrix():
    idx = np.arange(MXU_WIDTH) // HEAD_DIM
    return np.asarray((idx[:, None] == idx[None, :]) / HEAD_DIM, np.float32)


@functools.lru_cache(maxsize=None)
def _dft_tables(seq):
    def cs(n, rows, cols):
        ang = 2.0 * np.pi * ((rows[:, None] * cols[None, :]) % n).astype(np.float64) / n
        return np.cos(ang), np.sin(ang)
    k_block = np.arange(DFT_BLOCK, dtype=np.int64)
    k_chan = np.arange(FOURIER_GROUP, dtype=np.int64)
    c_block, s_block = cs(DFT_BLOCK, k_block, k_block)
    c_chan, s_chan = cs(FOURIER_GROUP, k_chan, k_chan)
    c_tw, s_tw = cs(seq, np.arange(seq // DFT_BLOCK, dtype=np.int64), k_block)
    lanes = lambda t: np.repeat(t[:, :, None], LANES, axis=2)
    tables = (np.concatenate([c_block, -s_block], axis=0), np.concatenate([c_chan, s_chan], axis=0),
              lanes(c_tw), lanes(s_tw))
    return tuple(np.asarray(t, np.float32) for t in tables)


def _mod_kernel(c_ref, w_ref, b_ref, o_ref):
    c = c_ref[...]
    s = c * jax.nn.sigmoid(c)
    s_hi = s.astype(BF16)
    s_lo = (s - s_hi.astype(F32)).astype(BF16)
    w = w_ref[...]
    w_hi = w.astype(BF16)
    w_lo = (w - w_hi.astype(F32)).astype(BF16)
    o_ref[...] = _dot(s_hi, w_hi) + _dot(s_lo, w_hi) + _dot(s_hi, w_lo) + b_ref[...]


def _modulation(cond, w_ada, b_ada):
    depth = w_ada.shape[0]
    width = N_MOD * D_MODEL
    out = pl.pallas_call(
        _mod_kernel,
        out_shape=jax.ShapeDtypeStruct((depth, MOD_ROWS, width), F32),
        grid=(depth, width // MOD_COL_TILE),
        in_specs=[
            pl.BlockSpec((MOD_ROWS, D_MODEL), lambda l, n: (0, 0)),
            pl.BlockSpec((None, D_MODEL, MOD_COL_TILE), lambda l, n: (l, 0, n)),
            pl.BlockSpec((None, 1, MOD_COL_TILE), lambda l, n: (l, 0, n)),
        ],
        out_specs=pl.BlockSpec((None, MOD_ROWS, MOD_COL_TILE), lambda l, n: (l, 0, n)),
        compiler_params=_params(2),
        name="adaln_modulation",
    )(cond, w_ada, b_ada.reshape(depth, 1, width))
    return out.reshape(depth, MOD_ROWS, N_MOD, D_MODEL)


def _ffn_kernel(x_ref, mod_ref, g_ref, w13_ref, w2_ref, o_ref, h_scr, acc_scr, *, mod_first):
    x = x_ref[...]
    h_scr[...] = _modulated_norm(x, g_ref[...], mod_ref, mod_first).astype(BF16)
    acc_scr[...] = jnp.zeros_like(acc_scr)

    for j in range(D_FF // FF_CHUNK):
        cols = slice(FF_CHUNK * j, FF_CHUNK * (j + 1))
        h = h_scr[...]
        gate = _dot(h, w13_ref[:, cols])
        up = _dot(h, w13_ref[:, D_FF + FF_CHUNK * j:D_FF + FF_CHUNK * (j + 1)])
        act = (gate * jax.nn.sigmoid(gate)) * up
        acc_scr[...] += _dot(act.astype(BF16), w2_ref[cols, :])

    o_ref[...] = x + (0.5 * mod_ref[mod_first + 2:mod_first + 3, :]) * acc_scr[...]


def _ffn(x, p, layer, half, mod_row):
    n = x.shape[0]
    tile = FFN_TOKEN_TILE
    return pl.pallas_call(
        functools.partial(_ffn_kernel, mod_first=6 * half),
        out_shape=jax.ShapeDtypeStruct((n, D_MODEL), F32),
        grid=(n // tile,),
        in_specs=[
            pl.BlockSpec((tile, D_MODEL), lambda i: (i, 0)),
            _mod_spec(layer, mod_row(tile)),
            _per_layer(p["norm_g"], layer, 2 * half),
            _per_layer(p["w13"], layer, half, resident=True),
            _per_layer(p["w2"], layer, half, resident=True),
        ],
        out_specs=pl.BlockSpec((tile, D_MODEL), lambda i: (i, 0)),
        scratch_shapes=[pltpu.VMEM((tile, D_MODEL), BF16), pltpu.VMEM((tile, D_MODEL), F32)],
        compiler_params=_params(1),
        name="swiglu_ffn",
    )(x, p["mod"], p["norm_g"], p["w13"], p["w2"])


def _mixer_in_kernel(*refs, rope, emit_state):
    refs = list(refs)
    x_ref, mod_ref, g_ref, win_ref, seg_ref, gq_ref, gk_ref = refs[:7]
    refs = refs[7:]
    if rope:
        cos_ref, sin_first_ref, sin_second_ref = refs[:3]
        refs = refs[3:]
    h_ref, q_ref, k_ref, v_ref, up_ref, uf_ref = refs[:6]
    refs = refs[6:]
    if emit_state:
        sk_ref, sv_ref = refs

    h = _modulated_norm(x_ref[...], g_ref[...], mod_ref, 3).astype(BF16)
    h_ref[...] = h

    def head_norm(t, gain):
        sq = (t * t).astype(BF16)
        width = seg_ref.shape[0]
        mean_sq = jnp.concatenate(
            [_dot(sq[:, width * j:width * (j + 1)], seg_ref[...]) for j in range(QK_WIDTH // width)],
            axis=1)
        return t * lax.rsqrt(mean_sq + EPS) * gain

    def head_chunks(t):
        return [t[:, LANES * i:LANES * (i + 1)] for i in range(N_HEADS)]

    def rotate(chunk):
        return (chunk * cos_ref[...]
                + pltpu.roll(chunk, LANES - HEAD_DIM // 4, axis=1) * sin_first_ref[...]
                + pltpu.roll(chunk, HEAD_DIM // 4, axis=1) * sin_second_ref[...])

    q = head_norm(_dot(h, win_ref[:, 0:QK_WIDTH]), gq_ref[...])
    for i, chunk in enumerate(head_chunks(q)):
        if rope:
            chunk = rotate(chunk)
        q_ref[:, LANES * i:LANES * (i + 1)] = (chunk * (HEAD_DIM ** -0.5 * LOG2_E)).astype(BF16)

    k = head_norm(_dot(h, win_ref[:, QK_WIDTH:2 * QK_WIDTH]), gk_ref[...])
    if emit_state:
        sk_ref[...] = k
    for i, chunk in enumerate(head_chunks(k)):
        if rope:
            chunk = rotate(chunk)
        k_ref[:, LANES * i:LANES * (i + 1)] = chunk.astype(BF16)

    v0 = 2 * QK_WIDTH
    v = _dot(h, win_ref[:, v0:v0 + ATTN_WIDTH])
    if emit_state:
        sv_ref[...] = v
    v_ref[...] = v.astype(BF16)

    p0 = v0 + ATTN_WIDTH
    up_ref[...] = _dot(h, win_ref[:, p0:p0 + POOL_WIDTH])
    f0 = p0 + POOL_WIDTH
    uf_ref[...] = _dot(h, win_ref[:, f0:f0 + FOURIER_WIDTH]).astype(BF16)


def _mixer_in(x, p, layer, mod_row, rope_tables, *, seq, emit_state):
    n = x.shape[0]
    rope = rope_tables is not None
    tiles_per_seq = seq // TOKEN_TILE
    tok = lambda width: pl.BlockSpec((TOKEN_TILE, width), lambda i: (i, 0))
    in_specs = [
        tok(D_MODEL),
        _mod_spec(layer, mod_row(TOKEN_TILE)),
        _per_layer(p["norm_g"], layer, 1),
        _per_layer(p["w_in"], layer, resident=True),
        _resident(p["seg"].shape, lambda i: (0, 0)),
        _per_layer(p["q_gain"], layer),
        _per_layer(p["k_gain"], layer),
    ]
    args = [x, p["mod"], p["norm_g"], p["w_in"], p["seg"], p["q_gain"], p["k_gain"]]
    if rope:
        in_specs += [pl.BlockSpec((TOKEN_TILE, LANES), lambda i: (i % tiles_per_seq, 0))] * 3
        args += list(rope_tables)
    out_shape = [jax.ShapeDtypeStruct((n, D_MODEL), BF16),
                 jax.ShapeDtypeStruct((n, QK_WIDTH), BF16),
                 jax.ShapeDtypeStruct((n, QK_WIDTH), BF16),
                 jax.ShapeDtypeStruct((n, ATTN_WIDTH), BF16),
                 jax.ShapeDtypeStruct((n, POOL_WIDTH), F32),
                 jax.ShapeDtypeStruct((n, FOURIER_WIDTH), BF16)]
    out_specs = [tok(D_MODEL), tok(QK_WIDTH), tok(QK_WIDTH), tok(ATTN_WIDTH),
                 tok(POOL_WIDTH), tok(FOURIER_WIDTH)]
    if emit_state:
        out_shape += [jax.ShapeDtypeStruct((n, QK_WIDTH), F32),
                      jax.ShapeDtypeStruct((n, ATTN_WIDTH), F32)]
        out_specs += [tok(QK_WIDTH), tok(ATTN_WIDTH)]
    return pl.pallas_call(
        functools.partial(_mixer_in_kernel, rope=rope, emit_state=emit_state),
        out_shape=out_shape,
        grid=(n // TOKEN_TILE,),
        in_specs=in_specs,
        out_specs=out_specs,
        compiler_params=_params(1),
        name="mixer_in",
    )(*args)


def _attention_kernel(*refs, n_past, lam_init, heads):
    refs = list(refs)
    q_ref, k_ref, v_ref = refs[:3]
    refs = refs[3:]
    if n_past:
        ck_ref, cv_ref = refs[:2]
        refs = refs[2:]
    lam_ref, gain_ref, o_ref = refs[:3]
    head_lanes = lambda j: slice(LANES * j, LANES * (j + 1))

    if n_past:
        k_all, v_all, k_past, v_past = refs[3:]
        group = pl.program_id(1)
        first_q = pl.program_id(2) == 0

        @pl.when(first_q & (group == 0))
        def _():
            for h in range(N_HEADS):
                k_past[h] = ck_ref[pl.ds(h, n_past, stride=N_HEADS), :].astype(BF16)
                v_past[h] = cv_ref[pl.ds(h, n_past, stride=N_HEADS), :].astype(BF16)

        @pl.when(first_q)
        def _():
            for j in range(heads):
                k_all[j, 0:n_past, :] = k_past[group * heads + j]
                v_all[j, 0:n_past, :] = v_past[group * heads + j]
                k_all[j, n_past:, :] = k_ref[:, head_lanes(j)]
                v_all[j, n_past:, :] = v_ref[:, head_lanes(j)]

        keys_of = lambda j, rows: k_all[j, rows, :]
        values_of = lambda j, rows: v_all[j, rows, :]
    else:
        keys_of = lambda j, rows: k_ref[rows, head_lanes(j)]
        values_of = lambda j, rows: v_ref[rows, head_lanes(j)]

    lq = lam_ref[...]
    lam = (jnp.exp(jnp.sum(lq[0:1, :] * lq[1:2, :], axis=-1, keepdims=True))
           - jnp.exp(jnp.sum(lq[2:3, :] * lq[3:4, :], axis=-1, keepdims=True)) + lam_init)

    n_q = q_ref.shape[0]
    n_keys = n_past + k_ref.shape[0]
    key_chunk = KEY_CHUNK if n_keys % KEY_CHUNK == 0 else n_keys
    lane = lax.broadcasted_iota(jnp.int32, (n_q, LANES), 1)
    zero = jnp.zeros((n_q, LANES), BF16)

    for j in range(heads):
        q = q_ref[:, head_lanes(j)]
        q_both = jnp.concatenate(
            [jnp.where(lane < HEAD_DIM, q, zero), jnp.where(lane >= HEAD_DIM, q, zero)], axis=0)

        run_max = acc = den = None
        for c in range(n_keys // key_chunk):
            rows = slice(key_chunk * c, key_chunk * (c + 1))
            s = _dot_nt(q_both, keys_of(j, rows))
            new_max = jnp.max(s, axis=-1, keepdims=True)
            if c:
                new_max = jnp.maximum(run_max, new_max)
            e = jnp.exp2(s - new_max)
            pv = _dot(e.astype(BF16), values_of(j, rows))
            part = e[:, 0:LANES]
            for t in range(1, key_chunk // LANES):
                part = part + e[:, LANES * t:LANES * (t + 1)]
            if c == 0:
                acc, den = pv, part
            else:
                alpha = jnp.exp2(run_max - new_max)
                acc, den = acc * alpha + pv, den * alpha + part
            run_max = new_max
        o_both = acc / jnp.sum(den, axis=-1, keepdims=True)

        o = o_both[0:n_q] - lam * o_both[n_q:]
        o = o * lax.rsqrt(jnp.mean(o * o, axis=-1, keepdims=True) + EPS) * gain_ref[...]
        o_ref[:, head_lanes(j)] = (o * (1.0 - lam_init)).astype(BF16)


def _attention(q, k, v, cache_k, cache_v, p, layer):
    batch, seq, _ = q.shape
    lam_init = 0.8 - 0.6 * math.exp(-0.3 * layer)
    n_past = 0 if cache_k is None else cache_k.shape[2] // N_HEADS
    n_keys = n_past + seq
    q_tile = min(Q_TILE, seq)
    heads = max(1, min(N_HEADS, ATTN_KEYS_PER_STEP // n_keys))
    width = LANES * heads
    seq_block = pl.BlockSpec((None, seq, width), lambda b, g, i: (b, 0, g))
    in_specs = [pl.BlockSpec((None, q_tile, width), lambda b, g, i: (b, i, g)), seq_block, seq_block]
    args = [q, k, v]
    scratch = []
    if n_past:
        past_block = pl.BlockSpec((None, None, n_past * N_HEADS, LANES), lambda b, g, i: (b, layer, 0, 0))
        in_specs += [past_block, past_block]
        args += [cache_k, cache_v]
        scratch = [pltpu.VMEM((heads, n_keys, LANES), BF16), pltpu.VMEM((heads, n_keys, V_DIM), BF16),
                   pltpu.VMEM((N_HEADS, n_past, LANES), BF16), pltpu.VMEM((N_HEADS, n_past, V_DIM), BF16)]
    in_specs += [_per_layer(p["lam_qk"], layer), _per_layer(p["subln_g"], layer)]
    args += [p["lam_qk"], p["subln_g"]]
    return pl.pallas_call(
        functools.partial(_attention_kernel, n_past=n_past, lam_init=lam_init, heads=heads),
        out_shape=jax.ShapeDtypeStruct((batch, seq, ATTN_WIDTH), BF16),
        grid=(batch, N_HEADS // heads, seq // q_tile),
        in_specs=in_specs,
        out_specs=pl.BlockSpec((None, q_tile, width), lambda b, g, i: (b, i, g)),
        scratch_shapes=scratch,
        compiler_params=_params(3),
        name="diff_attention",
    )(*args)


def _pool_kernel(u_ref, w_ref, scale_ref, o_ref):
    seq = u_ref.shape[0]
    row = lax.broadcasted_iota(jnp.int32, (seq, POOL_GROUP), 0)

    def shifted(x, k):
        keep = (row >= k) if k > 0 else (row < seq + k)
        return jnp.where(keep, pltpu.roll(x, k % seq, axis=0), 0.0)

    for g, window in enumerate(POOL_WINDOWS):
        half = window // 2
        lanes = slice(POOL_GROUP * g, POOL_GROUP * (g + 1))
        u = u_ref[:, lanes]
        ahead = behind = u
        n = 1
        while n < half:
            ahead = ahead + shifted(ahead, -n)
            behind = behind + shifted(behind, n)
            n *= 2
        win_sum = ahead + shifted(behind, 1)
        count = (jnp.minimum(row + half, seq) - jnp.maximum(row - half, 0)).astype(F32)
        d = win_sum / count - u
        o_ref[:, lanes] = (_dot(d.astype(BF16), w_ref[g]) * scale_ref[:, lanes]).astype(BF16)


def _pool(u, p, layer):
    batch, seq, _ = u.shape
    return pl.pallas_call(
        _pool_kernel,
        out_shape=jax.ShapeDtypeStruct((batch, seq, POOL_WIDTH), BF16),
        grid=(batch,),
        in_specs=[pl.BlockSpec((None, seq, POOL_WIDTH), lambda b: (b, 0, 0)),
                  _per_layer(p["w_pool"], layer),
                  _per_layer(p["pool_scale"], layer)],
        out_specs=pl.BlockSpec((None, seq, POOL_WIDTH), lambda b: (b, 0, 0)),
        compiler_params=_params(1),
        name="pool_mix",
    )(u, p["w_pool"], p["pool_scale"])


def _fft_across_blocks(blocks):
    n = len(blocks)
    if n == 1:
        return blocks
    even = _fft_across_blocks(blocks[0::2])
    odd = _fft_across_blocks(blocks[1::2])
    out = [None] * n
    for k in range(n // 2):
        o_re, o_im = odd[k]
        if k == 0:
            t_re, t_im = o_re, o_im
        elif 4 * k == n:
            t_re, t_im = o_im, -o_re
        else:
            c, s = math.cos(2 * math.pi * k / n), -math.sin(2 * math.pi * k / n)
            t_re, t_im = o_re * c - o_im * s, o_re * s + o_im * c
        e_re, e_im = even[k]
        out[k] = (e_re + t_re, e_im + t_im)
        out[k + n // 2] = (e_re - t_re, e_im - t_im)
    return out


def _fourier_kernel(*refs, radix):
    if radix > 1:
        x_ref, block_ref, chan_ref, tw_cos_ref, tw_sin_ref, o_ref, x32, w_scr = refs
        for g in range(FOURIER_GROUPS):
            x32[g] = x_ref[:, FOURIER_GROUP * g:FOURIER_GROUP * (g + 1)].astype(F32)
    else:
        x_ref, block_ref, chan_ref, o_ref, w_scr = refs
    seq = x_ref.shape[0]
    m = seq // radix

    blocks = []
    for j in range(radix):
        if radix > 1:
            x_j = jnp.concatenate([x32[g, pl.ds(j, m, stride=radix), :] for g in range(FOURIER_GROUPS)],
                                  axis=1).astype(BF16)
        else:
            x_j = x_ref[...]
        g = _dot(block_ref[...], x_j)
        g_re, g_im = g[0:m], g[m:2 * m]
        if j:
            c = jnp.concatenate([tw_cos_ref[j]] * FOURIER_GROUPS, axis=1)
            s = jnp.concatenate([tw_sin_ref[j]] * FOURIER_GROUPS, axis=1)
            g_re, g_im = g_re * c + g_im * s, g_im * c - g_re * s
        blocks.append((g_re, g_im))

    for k1, (w_re, w_im) in enumerate(_fft_across_blocks(blocks)):
        rows = slice(m * k1, m * (k1 + 1))
        w_scr[rows, 0:FOURIER_WIDTH] = w_re.astype(BF16)
        w_scr[rows, FOURIER_WIDTH:] = w_im.astype(BF16)

    scale = 1.0 / math.sqrt(seq * FOURIER_GROUP)
    for g in range(FOURIER_GROUPS):
        lanes = slice(FOURIER_GROUP * g, FOURIER_GROUP * (g + 1))
        im_lanes = slice(FOURIER_WIDTH + FOURIER_GROUP * g, FOURIER_WIDTH + FOURIER_GROUP * (g + 1))
        w = jnp.concatenate([w_scr[:, lanes], w_scr[:, im_lanes]], axis=1)
        o_ref[:, lanes] = (_dot(w, chan_ref[...]) * scale).astype(BF16)


def _fourier(x, tables):
    batch, seq, _ = x.shape
    radix = seq // DFT_BLOCK
    assert radix * DFT_BLOCK == seq and radix & (radix - 1) == 0
    block, chan, tw_cos, tw_sin = tables
    whole = lambda t: pl.BlockSpec(t.shape, lambda b: (0,) * t.ndim)
    in_specs = [pl.BlockSpec((None, seq, FOURIER_WIDTH), lambda b: (b, 0, 0)), whole(block), whole(chan)]
    args = [x, block, chan]
    scratch = [pltpu.VMEM((seq, 2 * FOURIER_WIDTH), BF16)]
    if radix > 1:
        in_specs += [whole(tw_cos), whole(tw_sin)]
        args += [tw_cos, tw_sin]
        scratch = [pltpu.VMEM((FOURIER_GROUPS, seq, FOURIER_GROUP), F32)] + scratch
    return pl.pallas_call(
        functools.partial(_fourier_kernel, radix=radix),
        out_shape=jax.ShapeDtypeStruct((batch, seq, FOURIER_WIDTH), BF16),
        grid=(batch,),
        in_specs=in_specs,
        out_specs=pl.BlockSpec((None, seq, FOURIER_WIDTH), lambda b: (b, 0, 0)),
        scratch_shapes=scratch,
        compiler_params=_params(1),
        name="fourier_mix",
    )(*args)


def _mixer_out_kernel(x_ref, mod_ref, h_ref, a_ref, p_ref, f_ref,
                      wg_ref, wpa_ref, wpp_ref, wpf_ref, wo_ref, o_ref):
    h = h_ref[...]

    def gated(branch_ref, w_ref, index):
        gate = jax.nn.sigmoid(_dot(h, wg_ref[:, D_MODEL * index:D_MODEL * (index + 1)]))
        return gate * _dot(branch_ref[...], w_ref[...])

    m = gated(a_ref, wpa_ref, 0) + gated(p_ref, wpp_ref, 1) + gated(f_ref, wpf_ref, 2)
    y = _dot(m.astype(BF16), wo_ref[...])
    o_ref[...] = x_ref[...] + mod_ref[5:6, :] * y


def _mixer_out(x, p, layer, mod_row, h, attn, pooled, fourier):
    n = x.shape[0]
    tok = lambda width: pl.BlockSpec((TOKEN_TILE, width), lambda i: (i, 0))
    names = ["w_gate", "w_pa", "w_pp", "w_pf", "w_out"]
    return pl.pallas_call(
        _mixer_out_kernel,
        out_shape=jax.ShapeDtypeStruct((n, D_MODEL), F32),
        grid=(n // TOKEN_TILE,),
        in_specs=[tok(D_MODEL), _mod_spec(layer, mod_row(TOKEN_TILE)),
                  tok(D_MODEL), tok(ATTN_WIDTH), tok(POOL_WIDTH), tok(FOURIER_WIDTH)]
                 + [_per_layer(p[name], layer, resident=True) for name in names],
        out_specs=tok(D_MODEL),
        compiler_params=_params(1),
        name="mixer_out",
    )(x, p["mod"], h, attn, pooled, fourier, *[p[name] for name in names])


def _trunk_layer(x, batch, seq, p, layer, mod_row, cache, rope_tables, dft_tables):
    x = _ffn(x, p, layer, 0, mod_row)

    emit_state = cache is None
    outs = _mixer_in(x, p, layer, mod_row, rope_tables, seq=seq, emit_state=emit_state)
    h, q, k, v, up, uf = outs[:6]
    per_seq = lambda t: t.reshape(batch, seq, t.shape[-1])
    cache_k, cache_v = (None, None) if emit_state else cache
    attn = _attention(per_seq(q), per_seq(k), per_seq(v), cache_k, cache_v, p, layer)
    pooled = _pool(per_seq(up), p, layer)
    fourier = _fourier(per_seq(uf), dft_tables)
    flat = lambda t: t.reshape(batch * seq, t.shape[-1])
    x = _mixer_out(x, p, layer, mod_row, h, flat(attn), flat(pooled), flat(fourier))

    x = _ffn(x, p, layer, 1, mod_row)
    state = (outs[6], outs[7]) if emit_state else (None, None)
    return x, state


def kernel(x_prompt, x_sample, cache_k, cache_v, c, c_ctx, w_ada, b_ada, norm_g, ffn_w13, ffn_w2, w_in, q_norm_g, k_norm_g, lam_qk, subln_g, w_pool, pool_scale, w_gate, w_pa, w_pp, w_pf, w_out):
    batch, seq, _ = x_prompt.shape
    dec_batch, dec_seq, _ = x_sample.shape
    depth = w_ada.shape[0]
    n_past = cache_k.shape[2]
    assert dec_batch + 1 <= MOD_ROWS
    for tile in (TOKEN_TILE, FFN_TOKEN_TILE):
        assert dec_seq % tile == 0 and (batch * seq) % tile == 0

    cond = jnp.concatenate(
        [c, c_ctx[None], jnp.zeros((MOD_ROWS - dec_batch - 1, D_MODEL), F32)], axis=0)

    cast = lambda t: t.astype(BF16)
    rows = lambda t: t.reshape(t.shape[:-1] + (1, t.shape[-1]))
    tile_gain = lambda g: rows(jnp.tile(g, (1, QK_WIDTH // HEAD_DIM)))
    p = {
        "mod": _modulation(cond, w_ada, b_ada),
        "norm_g": rows(norm_g), "w13": cast(ffn_w13), "w2": cast(ffn_w2), "w_in": cast(w_in),
        "seg": cast(jnp.asarray(_segment_mean_matrix())),
        "q_gain": tile_gain(q_norm_g), "k_gain": tile_gain(k_norm_g),
        "lam_qk": lam_qk, "subln_g": rows(subln_g),
        "w_pool": cast(w_pool), "pool_scale": rows(pool_scale),
        "w_gate": cast(w_gate), "w_pa": cast(w_pa), "w_pp": cast(w_pp), "w_pf": cast(w_pf),
        "w_out": cast(w_out),
    }
    def dft_tables(n):
        block, chan, tw_cos, tw_sin = (jnp.asarray(t) for t in _dft_tables(n))
        return cast(block), cast(chan), tw_cos, tw_sin
    dft_ctx, dft_lat = dft_tables(seq), dft_tables(dec_seq)
    rope_lat = tuple(jnp.asarray(t) for t in _rope_tables(dec_seq))
    cache = (cache_k.reshape(dec_batch, depth, n_past * N_HEADS, 2 * HEAD_DIM),
             cache_v.reshape(dec_batch, depth, n_past * N_HEADS, V_DIM))

    ctx_row = lambda tile: (lambda i: dec_batch)
    lat_row = lambda tile: (lambda i: (i * tile) // dec_seq)

    xp = x_prompt.reshape(batch * seq, D_MODEL)
    xs = x_sample.reshape(dec_batch * dec_seq, D_MODEL)
    new_k, new_v = [], []
    for l in range(depth):
        xp, (k_l, v_l) = _trunk_layer(xp, batch, seq, p, l, ctx_row, None, None, dft_ctx)
        new_k.append(k_l.reshape(batch, seq, N_HEADS, 2 * HEAD_DIM))
        new_v.append(v_l.reshape(batch, seq, N_HEADS, V_DIM))
        xs, _ = _trunk_layer(xs, dec_batch, dec_seq, p, l, lat_row, cache, rope_lat, dft_lat)

    return (xp.reshape(batch, seq, D_MODEL), xs.reshape(dec_batch, dec_seq, D_MODEL),
            jnp.stack(new_k, axis=1), jnp.stack(new_v, axis=1))
```

```python
import functools
import math

import numpy as np
import jax
import jax.numpy as jnp
from jax import lax
from jax.experimental import pallas as pl
from jax.experimental.pallas import tpu as pltpu

D_MODEL = 1024
N_HEADS = 8
HEAD_DIM = 64
V_DIM = 2 * HEAD_DIM
QK_WIDTH = N_HEADS * 2 * HEAD_DIM
ATTN_WIDTH = N_HEADS * V_DIM
POOL_WINDOWS = (2, 4, 8, 16)
POOL_GROUP = 128
POOL_WIDTH = POOL_GROUP * len(POOL_WINDOWS)
FOURIER_GROUPS = 4
FOURIER_GROUP = 128
FOURIER_WIDTH = FOURIER_GROUPS * FOURIER_GROUP
D_FF = 2816
N_MOD = 9
GRID_W = 64
ROPE_BASE = 10000.0
EPS = 1e-6

LANES = 128
MXU_WIDTH = 256
VMEM_LIMIT_BYTES = 56 * 1024 * 1024

MOD_ROWS = 16
MOD_COL_TILE = 1152
FF_CHUNK = 256
TOKEN_TILE = 512
FFN_TOKEN_TILE = 1024
Q_TILE = 2048
KEY_CHUNK = 256
ATTN_KEYS_PER_STEP = 4096
LOG2_E = math.log2(math.e)
DFT_BLOCK = MXU_WIDTH

F32 = jnp.float32
BF16 = jnp.bfloat16


def _dot(a, b):
    return jnp.dot(a, b, preferred_element_type=F32)


def _dot_nt(a, b):
    return lax.dot_general(a, b, (((1,), (1,)), ((), ())), preferred_element_type=F32)


def _resident(block_shape, index_map):
    return pl.BlockSpec(block_shape, index_map, pipeline_mode=pl.Buffered(1))


def _per_layer(array, *lead, resident=False):
    tail = array.shape[len(lead):]
    index = tuple(lead) + (0,) * len(tail)
    make = _resident if resident else pl.BlockSpec
    return make((None,) * len(lead) + tail, lambda *_: index)


def _mod_spec(layer, mod_row):
    return pl.BlockSpec((None, None, N_MOD, D_MODEL), lambda i: (layer, mod_row(i), 0, 0))


def _params(n_axes):
    return pltpu.CompilerParams(
        dimension_semantics=("arbitrary",) * n_axes, vmem_limit_bytes=VMEM_LIMIT_BYTES)


def _modulated_norm(x, gain, mod_ref, first):
    y = x * lax.rsqrt(jnp.mean(x * x, axis=-1, keepdims=True) + EPS) * gain
    return y * (1.0 + mod_ref[first + 1:first + 2, :]) + mod_ref[first:first + 1, :]


@functools.lru_cache(maxsize=None)
def _rope_tables(seq):
    n_freq = HEAD_DIM // 4
    lane = np.arange(LANES)
    within = lane % HEAD_DIM
    axis = within // (2 * n_freq)
    half = (within % (2 * n_freq)) // n_freq
    freq = lane % n_freq
    inv = ROPE_BASE ** (-np.arange(n_freq, dtype=np.float64) / n_freq)
    pos = np.arange(seq)
    coord = np.where(axis[None, :] == 0, (pos // GRID_W)[:, None], (pos % GRID_W)[:, None])
    ang = coord.astype(np.float64) * inv[freq][None, :]
    cos, sin = np.cos(ang), np.sin(ang)
    sin_first = np.where(half[None, :] == 0, -sin, 0.0)
    sin_second = np.where(half[None, :] == 1, sin, 0.0)
    return tuple(np.asarray(t, np.float32) for t in (cos, sin_first, sin_second))


@functools.lru_cache(maxsize=None)
def _segment_mean_matrix():
    idx = np.arange(MXU_WIDTH) // HEAD_DIM
    return np.asarray((idx[:, None] == idx[None, :]) / HEAD_DIM, np.float32)


@functools.lru_cache(maxsize=None)
def _dft_tables(seq):
    def cs(n, rows, cols):
        ang = 2.0 * np.pi * ((rows[:, None] * cols[None, :]) % n).astype(np.float64) / n
        return np.cos(ang), np.sin(ang)
    k_block = np.arange(DFT_BLOCK, dtype=np.int64)
    k_chan = np.arange(FOURIER_GROUP, dtype=np.int64)
    c_block, s_block = cs(DFT_BLOCK, k_block, k_block)
    c_chan, s_chan = cs(FOURIER_GROUP, k_chan, k_chan)
    c_tw, s_tw = cs(seq, np.arange(seq // DFT_BLOCK, dtype=np.int64), k_block)
    lanes = lambda t: np.repeat(t[:, :, None], LANES, axis=2)
    tables = (np.concatenate([c_block, -s_block], axis=0), np.concatenate([c_chan, s_chan], axis=0),
              lanes(c_tw), lanes(s_tw))
    return tuple(np.asarray(t, np.float32) for t in tables)


def _mod_kernel(c_ref, w_ref, b_ref, o_ref):
    c = c_ref[...]
    s = c * jax.nn.sigmoid(c)
    s_hi = s.astype(BF16)
    s_lo = (s - s_hi.astype(F32)).astype(BF16)
    w = w_ref[...]
    w_hi = w.astype(BF16)
    w_lo = (w - w_hi.astype(F32)).astype(BF16)
    o_ref[...] = _dot(s_hi, w_hi) + _dot(s_lo, w_hi) + _dot(s_hi, w_lo) + b_ref[...]


def _modulation(cond, w_ada, b_ada):
    depth = w_ada.shape[0]
    width = N_MOD * D_MODEL
    out = pl.pallas_call(
        _mod_kernel,
        out_shape=jax.ShapeDtypeStruct((depth, MOD_ROWS, width), F32),
        grid=(depth, width // MOD_COL_TILE),
        in_specs=[
            pl.BlockSpec((MOD_ROWS, D_MODEL), lambda l, n: (0, 0)),
            pl.BlockSpec((None, D_MODEL, MOD_COL_TILE), lambda l, n: (l, 0, n)),
            pl.BlockSpec((None, 1, MOD_COL_TILE), lambda l, n: (l, 0, n)),
        ],
        out_specs=pl.BlockSpec((None, MOD_ROWS, MOD_COL_TILE), lambda l, n: (l, 0, n)),
        compiler_params=_params(2),
        name="adaln_modulation",
    )(cond, w_ada, b_ada.reshape(depth, 1, width))
    return out.reshape(depth, MOD_ROWS, N_MOD, D_MODEL)


def _ffn_block(x, mod_ref, g_ref, w13_ref, w2_ref, h_scr, acc_scr, mod_first):
    h_scr[...] = _modulated_norm(x, g_ref[...], mod_ref, mod_first).astype(BF16)
    acc_scr[...] = jnp.zeros_like(acc_scr)

    for j in range(D_FF // FF_CHUNK):
        cols = slice(FF_CHUNK * j, FF_CHUNK * (j + 1))
        h = h_scr[...]
        gate = _dot(h, w13_ref[:, cols])
        up = _dot(h, w13_ref[:, D_FF + FF_CHUNK * j:D_FF + FF_CHUNK * (j + 1)])
        act = (gate * jax.nn.sigmoid(gate)) * up
        acc_scr[...] += _dot(act.astype(BF16), w2_ref[cols, :])

    return x + (0.5 * mod_ref[mod_first + 2:mod_first + 3, :]) * acc_scr[...]


def _ffn_kernel(x_ref, mod_ref, g_ref, w13_ref, w2_ref, o_ref, h_scr, acc_scr):
    o_ref[...] = _ffn_block(x_ref[...], mod_ref, g_ref, w13_ref, w2_ref, h_scr, acc_scr, 0)


def _ffn(x, p, layer, mod_row):
    n = x.shape[0]
    half = 0
    tile = FFN_TOKEN_TILE
    return pl.pallas_call(
        _ffn_kernel,
        out_shape=jax.ShapeDtypeStruct((n, D_MODEL), F32),
        grid=(n // tile,),
        in_specs=[
            pl.BlockSpec((tile, D_MODEL), lambda i: (i, 0)),
            _mod_spec(layer, mod_row(tile)),
            _per_layer(p["norm_g"], layer, 2 * half),
            _per_layer(p["w13"], layer, half, resident=True),
            _per_layer(p["w2"], layer, half, resident=True),
        ],
        out_specs=pl.BlockSpec((tile, D_MODEL), lambda i: (i, 0)),
        scratch_shapes=[pltpu.VMEM((tile, D_MODEL), BF16), pltpu.VMEM((tile, D_MODEL), F32)],
        compiler_params=_params(1),
        name="swiglu_ffn",
    )(x, p["mod"], p["norm_g"], p["w13"], p["w2"])


def _mixer_in_kernel(*refs, rope, emit_state):
    refs = list(refs)
    x_ref, mod_ref, g_ref, win_ref, seg_ref, gq_ref, gk_ref = refs[:7]
    refs = refs[7:]
    if rope:
        cos_ref, sin_first_ref, sin_second_ref = refs[:3]
        refs = refs[3:]
    h_ref, q_ref, k_ref, v_ref, up_ref, uf_ref = refs[:6]
    refs = refs[6:]
    if emit_state:
        sk_ref, sv_ref = refs

    h = _modulated_norm(x_ref[...], g_ref[...], mod_ref, 3).astype(BF16)
    h_ref[...] = h

    def head_norm(t, gain):
        sq = (t * t).astype(BF16)
        width = seg_ref.shape[0]
        mean_sq = jnp.concatenate(
            [_dot(sq[:, width * j:width * (j + 1)], seg_ref[...]) for j in range(QK_WIDTH // width)],
            axis=1)
        return t * lax.rsqrt(mean_sq + EPS) * gain

    def head_chunks(t):
        return [t[:, LANES * i:LANES * (i + 1)] for i in range(N_HEADS)]

    def rotate(chunk):
        return (chunk * cos_ref[...]
                + pltpu.roll(chunk, LANES - HEAD_DIM // 4, axis=1) * sin_first_ref[...]
                + pltpu.roll(chunk, HEAD_DIM // 4, axis=1) * sin_second_ref[...])

    q = head_norm(_dot(h, win_ref[:, 0:QK_WIDTH]), gq_ref[...])
    for i, chunk in enumerate(head_chunks(q)):
        if rope:
            chunk = rotate(chunk)
        q_ref[:, LANES * i:LANES * (i + 1)] = (chunk * (HEAD_DIM ** -0.5 * LOG2_E)).astype(BF16)

    k = head_norm(_dot(h, win_ref[:, QK_WIDTH:2 * QK_WIDTH]), gk_ref[...])
    if emit_state:
        sk_ref[...] = k
    for i, chunk in enumerate(head_chunks(k)):
        if rope:
            chunk = rotate(chunk)
        k_ref[:, LANES * i:LANES * (i + 1)] = chunk.astype(BF16)

    v0 = 2 * QK_WIDTH
    v = _dot(h, win_ref[:, v0:v0 + ATTN_WIDTH])
    if emit_state:
        sv_ref[...] = v
    v_ref[...] = v.astype(BF16)

    p0 = v0 + ATTN_WIDTH
    up_ref[...] = _dot(h, win_ref[:, p0:p0 + POOL_WIDTH])
    f0 = p0 + POOL_WIDTH
    uf_ref[...] = _dot(h, win_ref[:, f0:f0 + FOURIER_WIDTH]).astype(BF16)


def _mixer_in(x, p, layer, mod_row, rope_tables, *, seq, emit_state):
    n = x.shape[0]
    rope = rope_tables is not None
    tiles_per_seq = seq // TOKEN_TILE
    tok = lambda width: pl.BlockSpec((TOKEN_TILE, width), lambda i: (i, 0))
    in_specs = [
        tok(D_MODEL),
        _mod_spec(layer, mod_row(TOKEN_TILE)),
        _per_layer(p["norm_g"], layer, 1),
        _per_layer(p["w_in"], layer, resident=True),
        _resident(p["seg"].shape, lambda i: (0, 0)),
        _per_layer(p["q_gain"], layer),
        _per_layer(p["k_gain"], layer),
    ]
    args = [x, p["mod"], p["norm_g"], p["w_in"], p["seg"], p["q_gain"], p["k_gain"]]
    if rope:
        in_specs += [pl.BlockSpec((TOKEN_TILE, LANES), lambda i: (i % tiles_per_seq, 0))] * 3
        args += list(rope_tables)
    out_shape = [jax.ShapeDtypeStruct((n, D_MODEL), BF16),
                 jax.ShapeDtypeStruct((n, QK_WIDTH), BF16),
                 jax.ShapeDtypeStruct((n, QK_WIDTH), BF16),
                 jax.ShapeDtypeStruct((n, ATTN_WIDTH), BF16),
                 jax.ShapeDtypeStruct((n, POOL_WIDTH), F32),
                 jax.ShapeDtypeStruct((n, FOURIER_WIDTH), BF16)]
    out_specs = [tok(D_MODEL), tok(QK_WIDTH), tok(QK_WIDTH), tok(ATTN_WIDTH),
                 tok(POOL_WIDTH), tok(FOURIER_WIDTH)]
    if emit_state:
        out_shape += [jax.ShapeDtypeStruct((n, QK_WIDTH), F32),
                      jax.ShapeDtypeStruct((n, ATTN_WIDTH), F32)]
        out_specs += [tok(QK_WIDTH), tok(ATTN_WIDTH)]
    return pl.pallas_call(
        functools.partial(_mixer_in_kernel, rope=rope, emit_state=emit_state),
        out_shape=out_shape,
        grid=(n // TOKEN_TILE,),
        in_specs=in_specs,
        out_specs=out_specs,
        compiler_params=_params(1),
        name="mixer_in",
    )(*args)


def _attention_kernel(*refs, n_past, lam_init, heads):
    refs = list(refs)
    q_ref, k_ref, v_ref = refs[:3]
    refs = refs[3:]
    if n_past:
        ck_ref, cv_ref = refs[:2]
        refs = refs[2:]
    lam_ref, gain_ref, o_ref = refs[:3]
    head_lanes = lambda j: slice(LANES * j, LANES * (j + 1))

    if n_past:
        k_all, v_all, k_past, v_past = refs[3:]
        group = pl.program_id(1)
        first_q = pl.program_id(2) == 0

        @pl.when(first_q & (group == 0))
        def _():
            for h in range(N_HEADS):
                k_past[h] = ck_ref[pl.ds(h, n_past, stride=N_HEADS), :].astype(BF16)
                v_past[h] = cv_ref[pl.ds(h, n_past, stride=N_HEADS), :].astype(BF16)

        @pl.when(first_q)
        def _():
            for j in range(heads):
                k_all[j, 0:n_past, :] = k_past[group * heads + j]
                v_all[j, 0:n_past, :] = v_past[group * heads + j]
                k_all[j, n_past:, :] = k_ref[:, head_lanes(j)]
                v_all[j, n_past:, :] = v_ref[:, head_lanes(j)]

        keys_of = lambda j, rows: k_all[j, rows, :]
        values_of = lambda j, rows: v_all[j, rows, :]
    else:
        keys_of = lambda j, rows: k_ref[rows, head_lanes(j)]
        values_of = lambda j, rows: v_ref[rows, head_lanes(j)]

    lq = lam_ref[...]
    lam = (jnp.exp(jnp.sum(lq[0:1, :] * lq[1:2, :], axis=-1, keepdims=True))
           - jnp.exp(jnp.sum(lq[2:3, :] * lq[3:4, :], axis=-1, keepdims=True)) + lam_init)

    n_q = q_ref.shape[0]
    n_keys = n_past + k_ref.shape[0]
    key_chunk = KEY_CHUNK if n_keys % KEY_CHUNK == 0 else n_keys
    lane = lax.broadcasted_iota(jnp.int32, (n_q, LANES), 1)
    zero = jnp.zeros((n_q, LANES), BF16)

    for j in range(heads):
        q = q_ref[:, head_lanes(j)]
        q_both = jnp.concatenate(
            [jnp.where(lane < HEAD_DIM, q, zero), jnp.where(lane >= HEAD_DIM, q, zero)], axis=0)

        run_max = acc = den = None
        for c in range(n_keys // key_chunk):
            rows = slice(key_chunk * c, key_chunk * (c + 1))
            s = _dot_nt(q_both, keys_of(j, rows))
            new_max = jnp.max(s, axis=-1, keepdims=True)
            if c:
                new_max = jnp.maximum(run_max, new_max)
            e = jnp.exp2(s - new_max)
            pv = _dot(e.astype(BF16), values_of(j, rows))
            part = e[:, 0:LANES]
            for t in range(1, key_chunk // LANES):
                part = part + e[:, LANES * t:LANES * (t + 1)]
            if c == 0:
                acc, den = pv, part
            else:
                alpha = jnp.exp2(run_max - new_max)
                acc, den = acc * alpha + pv, den * alpha + part
            run_max = new_max
        o_both = acc / jnp.sum(den, axis=-1, keepdims=True)

        o = o_both[0:n_q] - lam * o_both[n_q:]
        o = o * lax.rsqrt(jnp.mean(o * o, axis=-1, keepdims=True) + EPS) * gain_ref[...]
        o_ref[:, head_lanes(j)] = (o * (1.0 - lam_init)).astype(BF16)


def _attention(q, k, v, cache_k, cache_v, p, layer):
    batch, seq, _ = q.shape
    lam_init = 0.8 - 0.6 * math.exp(-0.3 * layer)
    n_past = 0 if cache_k is None else cache_k.shape[2] // N_HEADS
    n_keys = n_past + seq
    q_tile = min(Q_TILE, seq)
    heads = max(1, min(N_HEADS, ATTN_KEYS_PER_STEP // n_keys))
    width = LANES * heads
    seq_block = pl.BlockSpec((None, seq, width), lambda b, g, i: (b, 0, g))
    in_specs = [pl.BlockSpec((None, q_tile, width), lambda b, g, i: (b, i, g)), seq_block, seq_block]
    args = [q, k, v]
    scratch = []
    if n_past:
        past_block = pl.BlockSpec((None, None, n_past * N_HEADS, LANES), lambda b, g, i: (b, layer, 0, 0))
        in_specs += [past_block, past_block]
        args += [cache_k, cache_v]
        scratch = [pltpu.VMEM((heads, n_keys, LANES), BF16), pltpu.VMEM((heads, n_keys, V_DIM), BF16),
                   pltpu.VMEM((N_HEADS, n_past, LANES), BF16), pltpu.VMEM((N_HEADS, n_past, V_DIM), BF16)]
    in_specs += [_per_layer(p["lam_qk"], layer), _per_layer(p["subln_g"], layer)]
    args += [p["lam_qk"], p["subln_g"]]
    return pl.pallas_call(
        functools.partial(_attention_kernel, n_past=n_past, lam_init=lam_init, heads=heads),
        out_shape=jax.ShapeDtypeStruct((batch, seq, ATTN_WIDTH), BF16),
        grid=(batch, N_HEADS // heads, seq // q_tile),
        in_specs=in_specs,
        out_specs=pl.BlockSpec((None, q_tile, width), lambda b, g, i: (b, i, g)),
        scratch_shapes=scratch,
        compiler_params=_params(3),
        name="diff_attention",
    )(*args)


def _pool_kernel(u_ref, w_ref, scale_ref, o_ref):
    seq = u_ref.shape[0]
    row = lax.broadcasted_iota(jnp.int32, (seq, POOL_GROUP), 0)

    def shifted(x, k):
        keep = (row >= k) if k > 0 else (row < seq + k)
        return jnp.where(keep, pltpu.roll(x, k % seq, axis=0), 0.0)

    for g, window in enumerate(POOL_WINDOWS):
        half = window // 2
        lanes = slice(POOL_GROUP * g, POOL_GROUP * (g + 1))
        u = u_ref[:, lanes]
        ahead = behind = u
        n = 1
        while n < half:
            ahead = ahead + shifted(ahead, -n)
            behind = behind + shifted(behind, n)
            n *= 2
        win_sum = ahead + shifted(behind, 1)
        count = (jnp.minimum(row + half, seq) - jnp.maximum(row - half, 0)).astype(F32)
        d = win_sum / count - u
        o_ref[:, lanes] = (_dot(d.astype(BF16), w_ref[g]) * scale_ref[:, lanes]).astype(BF16)


def _fft_across_blocks(blocks):
    n = len(blocks)
    if n == 1:
        return blocks
    even = _fft_across_blocks(blocks[0::2])
    odd = _fft_across_blocks(blocks[1::2])
    out = [None] * n
    for k in range(n // 2):
        o_re, o_im = odd[k]
        if k == 0:
            t_re, t_im = o_re, o_im
        elif 4 * k == n:
            t_re, t_im = o_im, -o_re
        else:
            c, s = math.cos(2 * math.pi * k / n), -math.sin(2 * math.pi * k / n)
            t_re, t_im = o_re * c - o_im * s, o_re * s + o_im * c
        e_re, e_im = even[k]
        out[k] = (e_re + t_re, e_im + t_im)
        out[k + n // 2] = (e_re - t_re, e_im - t_im)
    return out


def _fourier_kernel(*refs, radix):
    if radix > 1:
        x_ref, block_ref, chan_ref, tw_cos_ref, tw_sin_ref, o_ref, x32, w_scr = refs
        for g in range(FOURIER_GROUPS):
            x32[g] = x_ref[:, FOURIER_GROUP * g:FOURIER_GROUP * (g + 1)].astype(F32)
    else:
        x_ref, block_ref, chan_ref, o_ref, w_scr = refs
    seq = x_ref.shape[0]
    m = seq // radix

    blocks = []
    for j in range(radix):
        if radix > 1:
            x_j = jnp.concatenate([x32[g, pl.ds(j, m, stride=radix), :] for g in range(FOURIER_GROUPS)],
                                  axis=1).astype(BF16)
        else:
            x_j = x_ref[...]
        g = _dot(block_ref[...], x_j)
        g_re, g_im = g[0:m], g[m:2 * m]
        if j:
            c = jnp.concatenate([tw_cos_ref[j]] * FOURIER_GROUPS, axis=1)
            s = jnp.concatenate([tw_sin_ref[j]] * FOURIER_GROUPS, axis=1)
            g_re, g_im = g_re * c + g_im * s, g_im * c - g_re * s
        blocks.append((g_re, g_im))

    for k1, (w_re, w_im) in enumerate(_fft_across_blocks(blocks)):
        rows = slice(m * k1, m * (k1 + 1))
        w_scr[rows, 0:FOURIER_WIDTH] = w_re.astype(BF16)
        w_scr[rows, FOURIER_WIDTH:] = w_im.astype(BF16)

    scale = 1.0 / math.sqrt(seq * FOURIER_GROUP)
    for g in range(FOURIER_GROUPS):
        lanes = slice(FOURIER_GROUP * g, FOURIER_GROUP * (g + 1))
        im_lanes = slice(FOURIER_WIDTH + FOURIER_GROUP * g, FOURIER_WIDTH + FOURIER_GROUP * (g + 1))
        w = jnp.concatenate([w_scr[:, lanes], w_scr[:, im_lanes]], axis=1)
        o_ref[:, lanes] = (_dot(w, chan_ref[...]) * scale).astype(BF16)


def _sequence_mix_kernel(u_ref, w_pool_ref, scale_ref, *refs, radix):
    n_fourier_in = 5 if radix > 1 else 3
    pool_out_ref = refs[n_fourier_in]
    _pool_kernel(u_ref, w_pool_ref, scale_ref, pool_out_ref)
    _fourier_kernel(*refs[:n_fourier_in], *refs[n_fourier_in + 1:], radix=radix)


def _sequence_mix(u, x, p, layer, tables):
    batch, seq, _ = x.shape
    radix = seq // DFT_BLOCK
    assert radix * DFT_BLOCK == seq and radix & (radix - 1) == 0
    block, chan, tw_cos, tw_sin = tables
    whole = lambda t: pl.BlockSpec(t.shape, lambda b: (0,) * t.ndim)
    per_seq = lambda width: pl.BlockSpec((None, seq, width), lambda b: (b, 0, 0))
    in_specs = [per_seq(POOL_WIDTH), _per_layer(p["w_pool"], layer), _per_layer(p["pool_scale"], layer),
                per_seq(FOURIER_WIDTH), whole(block), whole(chan)]
    args = [u, p["w_pool"], p["pool_scale"], x, block, chan]
    scratch = [pltpu.VMEM((seq, 2 * FOURIER_WIDTH), BF16)]
    if radix > 1:
        in_specs += [whole(tw_cos), whole(tw_sin)]
        args += [tw_cos, tw_sin]
        scratch = [pltpu.VMEM((FOURIER_GROUPS, seq, FOURIER_GROUP), F32)] + scratch
    return pl.pallas_call(
        functools.partial(_sequence_mix_kernel, radix=radix),
        out_shape=[jax.ShapeDtypeStruct((batch, seq, POOL_WIDTH), BF16),
                   jax.ShapeDtypeStruct((batch, seq, FOURIER_WIDTH), BF16)],
        grid=(batch,),
        in_specs=in_specs,
        out_specs=[per_seq(POOL_WIDTH), per_seq(FOURIER_WIDTH)],
        scratch_shapes=scratch,
        compiler_params=_params(1),
        name="sequence_mix",
    )(*args)


def _mixer_out_kernel(x_ref, mod_ref, h_ref, a_ref, p_ref, f_ref,
                      wg_ref, wpa_ref, wpp_ref, wpf_ref, wo_ref, g_ref, w13_ref, w2_ref,
                      o_ref, h_scr, acc_scr):
    h = h_ref[...]

    def gated(branch_ref, w_ref, index):
        gate = jax.nn.sigmoid(_dot(h, wg_ref[:, D_MODEL * index:D_MODEL * (index + 1)]))
        return gate * _dot(branch_ref[...], w_ref[...])

    m = gated(a_ref, wpa_ref, 0) + gated(p_ref, wpp_ref, 1) + gated(f_ref, wpf_ref, 2)
    x = x_ref[...] + mod_ref[5:6, :] * _dot(m.astype(BF16), wo_ref[...])
    o_ref[...] = _ffn_block(x, mod_ref, g_ref, w13_ref, w2_ref, h_scr, acc_scr, 6)


def _mixer_out_ffn(x, p, layer, mod_row, h, attn, pooled, fourier):
    n = x.shape[0]
    tok = lambda width: pl.BlockSpec((TOKEN_TILE, width), lambda i: (i, 0))
    names = ["w_gate", "w_pa", "w_pp", "w_pf", "w_out"]
    return pl.pallas_call(
        _mixer_out_kernel,
        out_shape=jax.ShapeDtypeStruct((n, D_MODEL), F32),
        grid=(n // TOKEN_TILE,),
        in_specs=[tok(D_MODEL), _mod_spec(layer, mod_row(TOKEN_TILE)),
                  tok(D_MODEL), tok(ATTN_WIDTH), tok(POOL_WIDTH), tok(FOURIER_WIDTH)]
                 + [_per_layer(p[name], layer, resident=True) for name in names]
                 + [_per_layer(p["norm_g"], layer, 2),
                    _per_layer(p["w13"], layer, 1, resident=True),
                    _per_layer(p["w2"], layer, 1, resident=True)],
        out_specs=tok(D_MODEL),
        scratch_shapes=[pltpu.VMEM((TOKEN_TILE, D_MODEL), BF16), pltpu.VMEM((TOKEN_TILE, D_MODEL), F32)],
        compiler_params=_params(1),
        name="mixer_out_ffn",
    )(x, p["mod"], h, attn, pooled, fourier, *[p[name] for name in names],
      p["norm_g"], p["w13"], p["w2"])


def _trunk_layer(x, batch, seq, p, layer, mod_row, cache, rope_tables, dft_tables):
    x = _ffn(x, p, layer, mod_row)

    emit_state = cache is None
    outs = _mixer_in(x, p, layer, mod_row, rope_tables, seq=seq, emit_state=emit_state)
    h, q, k, v, up, uf = outs[:6]
    per_seq = lambda t: t.reshape(batch, seq, t.shape[-1])
    cache_k, cache_v = (None, None) if emit_state else cache
    attn = _attention(per_seq(q), per_seq(k), per_seq(v), cache_k, cache_v, p, layer)
    pooled, fourier = _sequence_mix(per_seq(up), per_seq(uf), p, layer, dft_tables)
    flat = lambda t: t.reshape(batch * seq, t.shape[-1])
    x = _mixer_out_ffn(x, p, layer, mod_row, h, flat(attn), flat(pooled), flat(fourier))
    state = (outs[6], outs[7]) if emit_state else (None, None)
    return x, state


def kernel(x_prompt, x_sample, cache_k, cache_v, c, c_ctx, w_ada, b_ada, norm_g, ffn_w13, ffn_w2, w_in, q_norm_g, k_norm_g, lam_qk, subln_g, w_pool, pool_scale, w_gate, w_pa, w_pp, w_pf, w_out):
    batch, seq, _ = x_prompt.shape
    dec_batch, dec_seq, _ = x_sample.shape
    depth = w_ada.shape[0]
    n_past = cache_k.shape[2]
    assert dec_batch + 1 <= MOD_ROWS
    for tile in (TOKEN_TILE, FFN_TOKEN_TILE):
        assert dec_seq % tile == 0 and (batch * seq) % tile == 0

    cond = jnp.concatenate(
        [c, c_ctx[None], jnp.zeros((MOD_ROWS - dec_batch - 1, D_MODEL), F32)], axis=0)

    cast = lambda t: t.astype(BF16)
    rows = lambda t: t.reshape(t.shape[:-1] + (1, t.shape[-1]))
    tile_gain = lambda g: rows(jnp.tile(g, (1, QK_WIDTH // HEAD_DIM)))
    p = {
        "mod": _modulation(cond, w_ada, b_ada),
        "norm_g": rows(norm_g), "w13": cast(ffn_w13), "w2": cast(ffn_w2), "w_in": cast(w_in),
        "seg": cast(jnp.asarray(_segment_mean_matrix())),
        "q_gain": tile_gain(q_norm_g), "k_gain": tile_gain(k_norm_g),
        "lam_qk": lam_qk, "subln_g": rows(subln_g),
        "w_pool": cast(w_pool), "pool_scale": rows(pool_scale),
        "w_gate": cast(w_gate), "w_pa": cast(w_pa), "w_pp": cast(w_pp), "w_pf": cast(w_pf),
        "w_out": cast(w_out),
    }
    def dft_tables(n):
        block, chan, tw_cos, tw_sin = (jnp.asarray(t) for t in _dft_tables(n))
        return cast(block), cast(chan), tw_cos, tw_sin
    dft_ctx, dft_lat = dft_tables(seq), dft_tables(dec_seq)
    rope_lat = tuple(jnp.asarray(t) for t in _rope_tables(dec_seq))
    cache = (cache_k.reshape(dec_batch, depth, n_past * N_HEADS, 2 * HEAD_DIM),
             cache_v.reshape(dec_batch, depth, n_past * N_HEADS, V_DIM))

    ctx_row = lambda tile: (lambda i: dec_batch)
    lat_row = lambda tile: (lambda i: (i * tile) // dec_seq)

    xp = x_prompt.reshape(batch * seq, D_MODEL)
    xs = x_sample.reshape(dec_batch * dec_seq, D_MODEL)
    new_k, new_v = [], []
    for l in range(depth):
        xp, (k_l, v_l) = _trunk_layer(xp, batch, seq, p, l, ctx_row, None, None, dft_ctx)
        new_k.append(k_l.reshape(batch, seq, N_HEADS, 2 * HEAD_DIM))
        new_v.append(v_l.reshape(batch, seq, N_HEADS, V_DIM))
        xs, _ = _trunk_layer(xs, dec_batch, dec_seq, p, l, lat_row, cache, rope_lat, dft_lat)

    return (xp.reshape(batch, seq, D_MODEL), xs.reshape(dec_batch, dec_seq, D_MODEL),
            jnp.stack(new_k, axis=1), jnp.stack(new_v, axis=1))
```

```python
import functools
import math

import numpy as np
import jax
import jax.numpy as jnp
from jax import lax
from jax.experimental import pallas as pl
from jax.experimental.pallas import tpu as pltpu

D_MODEL = 1024
N_HEADS = 8
HEAD_DIM = 64
V_DIM = 2 * HEAD_DIM
QK_WIDTH = N_HEADS * 2 * HEAD_DIM
ATTN_WIDTH = N_HEADS * V_DIM
POOL_WINDOWS = (2, 4, 8, 16)
POOL_GROUP = 128
POOL_WIDTH = POOL_GROUP * len(POOL_WINDOWS)
FOURIER_GROUPS = 4
FOURIER_GROUP = 128
FOURIER_WIDTH = FOURIER_GROUPS * FOURIER_GROUP
D_FF = 2816
N_MOD = 9
GRID_W = 64
ROPE_BASE = 10000.0
EPS = 1e-6

LANES = 128
MXU_WIDTH = 256
VMEM_LIMIT_BYTES = 56 * 1024 * 1024

MOD_ROWS = 16
MOD_COL_TILE = 1152
FF_CHUNK = 256
TOKEN_TILE = 512
FFN_TOKEN_TILE = 1024
Q_TILE = 2048
KEY_CHUNK = 256
ATTN_KEYS_PER_STEP = 4096
LOG2_E = math.log2(math.e)
DFT_BLOCK = MXU_WIDTH

F32 = jnp.float32
BF16 = jnp.bfloat16


def _dot(a, b):
    return jnp.dot(a, b, preferred_element_type=F32)


def _dot_nt(a, b):
    return lax.dot_general(a, b, (((1,), (1,)), ((), ())), preferred_element_type=F32)


def _resident(block_shape, index_map):
    return pl.BlockSpec(block_shape, index_map, pipeline_mode=pl.Buffered(1))


def _per_layer(array, *lead, resident=False):
    tail = array.shape[len(lead):]
    index = tuple(lead) + (0,) * len(tail)
    make = _resident if resident else pl.BlockSpec
    return make((None,) * len(lead) + tail, lambda *_: index)


def _mod_spec(layer, mod_row):
    return pl.BlockSpec((None, None, N_MOD, D_MODEL), lambda i: (layer, mod_row(i), 0, 0))


def _params(n_axes):
    return pltpu.CompilerParams(
        dimension_semantics=("arbitrary",) * n_axes, vmem_limit_bytes=VMEM_LIMIT_BYTES)


def _modulated_norm(x, gain, mod_ref, first):
    y = x * lax.rsqrt(jnp.mean(x * x, axis=-1, keepdims=True) + EPS) * gain
    return y * (1.0 + mod_ref[first + 1:first + 2, :]) + mod_ref[first:first + 1, :]


@functools.lru_cache(maxsize=None)
def _rope_tables(seq):
    n_freq = HEAD_DIM // 4
    lane = np.arange(LANES)
    within = lane % HEAD_DIM
    axis = within // (2 * n_freq)
    half = (within % (2 * n_freq)) // n_freq
    freq = lane % n_freq
    inv = ROPE_BASE ** (-np.arange(n_freq, dtype=np.float64) / n_freq)
    pos = np.arange(seq)
    coord = np.where(axis[None, :] == 0, (pos // GRID_W)[:, None], (pos % GRID_W)[:, None])
    ang = coord.astype(np.float64) * inv[freq][None, :]
    cos, sin = np.cos(ang), np.sin(ang)
    sin_first = np.where(half[None, :] == 0, -sin, 0.0)
    sin_second = np.where(half[None, :] == 1, sin, 0.0)
    return tuple(np.asarray(t, np.float32) for t in (cos, sin_first, sin_second))


@functools.lru_cache(maxsize=None)
def _segment_mean_matrix():
    idx = np.arange(MXU_WIDTH) // HEAD_DIM
    return np.asarray((idx[:, None] == idx[None, :]) / HEAD_DIM, np.float32)


@functools.lru_cache(maxsize=None)
def _dft_tables(seq):
    def cs(n, rows, cols):
        ang = 2.0 * np.pi * ((rows[:, None] * cols[None, :]) % n).astype(np.float64) / n
        return np.cos(ang), np.sin(ang)
    k_block = np.arange(DFT_BLOCK, dtype=np.int64)
    k_chan = np.arange(FOURIER_GROUP, dtype=np.int64)
    c_block, s_block = cs(DFT_BLOCK, k_block, k_block)
    c_chan, s_chan = cs(FOURIER_GROUP, k_chan, k_chan)
    c_tw, s_tw = cs(seq, np.arange(seq // DFT_BLOCK, dtype=np.int64), k_block)
    lanes = lambda t: np.repeat(t[:, :, None], LANES, axis=2)
    tables = (np.concatenate([c_block, -s_block], axis=0), np.concatenate([c_chan, s_chan], axis=0),
              lanes(c_tw), lanes(s_tw))
    return tuple(np.asarray(t, np.float32) for t in tables)


def _mod_kernel(c_ref, w_ref, b_ref, o_ref):
    c = c_ref[...]
    s = c * jax.nn.sigmoid(c)
    s_hi = s.astype(BF16)
    s_lo = (s - s_hi.astype(F32)).astype(BF16)
    w = w_ref[...]
    w_hi = w.astype(BF16)
    w_lo = (w - w_hi.astype(F32)).astype(BF16)
    o_ref[...] = _dot(s_hi, w_hi) + _dot(s_lo, w_hi) + _dot(s_hi, w_lo) + b_ref[...]


def _modulation(cond, w_ada, b_ada):
    depth = w_ada.shape[0]
    width = N_MOD * D_MODEL
    out = pl.pallas_call(
        _mod_kernel,
        out_shape=jax.ShapeDtypeStruct((depth, MOD_ROWS, width), F32),
        grid=(depth, width // MOD_COL_TILE),
        in_specs=[
            pl.BlockSpec((MOD_ROWS, D_MODEL), lambda l, n: (0, 0)),
            pl.BlockSpec((None, D_MODEL, MOD_COL_TILE), lambda l, n: (l, 0, n)),
            pl.BlockSpec((None, 1, MOD_COL_TILE), lambda l, n: (l, 0, n)),
        ],
        out_specs=pl.BlockSpec((None, MOD_ROWS, MOD_COL_TILE), lambda l, n: (l, 0, n)),
        compiler_params=_params(2),
        name="adaln_modulation",
    )(cond, w_ada, b_ada.reshape(depth, 1, width))
    return out.reshape(depth, MOD_ROWS, N_MOD, D_MODEL)


def _ffn_block(x, mod_ref, g_ref, w13_ref, w2_ref, h_scr, acc_scr, mod_first):
    h_scr[...] = _modulated_norm(x, g_ref[...], mod_ref, mod_first).astype(BF16)
    acc_scr[...] = jnp.zeros_like(acc_scr)

    for j in range(D_FF // FF_CHUNK):
        cols = slice(FF_CHUNK * j, FF_CHUNK * (j + 1))
        h = h_scr[...]
        gate = _dot(h, w13_ref[:, cols])
        up = _dot(h, w13_ref[:, D_FF + FF_CHUNK * j:D_FF + FF_CHUNK * (j + 1)])
        act = (gate * jax.nn.sigmoid(gate)) * up
        acc_scr[...] += _dot(act.astype(BF16), w2_ref[cols, :])

    return x + (0.5 * mod_ref[mod_first + 2:mod_first + 3, :]) * acc_scr[...]


def _ffn_kernel(x_ref, mod_ref, g_ref, w13_ref, w2_ref, o_ref, h_scr, acc_scr):
    o_ref[...] = _ffn_block(x_ref[...], mod_ref, g_ref, w13_ref, w2_ref, h_scr, acc_scr, 0)


def _ffn(x, p, layer, mod_row):
    n = x.shape[0]
    half = 0
    tile = FFN_TOKEN_TILE
    return pl.pallas_call(
        _ffn_kernel,
        out_shape=jax.ShapeDtypeStruct((n, D_MODEL), F32),
        grid=(n // tile,),
        in_specs=[
            pl.BlockSpec((tile, D_MODEL), lambda i: (i, 0)),
            _mod_spec(layer, mod_row(tile)),
            _per_layer(p["norm_g"], layer, 2 * half),
            _per_layer(p["w13"], layer, half, resident=True),
            _per_layer(p["w2"], layer, half, resident=True),
        ],
        out_specs=pl.BlockSpec((tile, D_MODEL), lambda i: (i, 0)),
        scratch_shapes=[pltpu.VMEM((tile, D_MODEL), BF16), pltpu.VMEM((tile, D_MODEL), F32)],
        compiler_params=_params(1),
        name="swiglu_ffn",
    )(x, p["mod"], p["norm_g"], p["w13"], p["w2"])


def _mixer_in_kernel(*refs, rope, state):
    refs = list(refs)
    x_ref, mod_ref, g_ref, win_ref, seg_ref, gq_ref, gk_ref = refs[:7]
    refs = refs[7:]
    if rope:
        cos_ref, sin_first_ref, sin_second_ref = refs[:3]
        refs = refs[3:]
    if state == "update":
        refs = refs[2:]
    h_ref, q_ref, k_ref, v_ref, up_ref, uf_ref = refs[:6]
    refs = refs[6:]
    if state:
        sk_ref, sv_ref = refs

    def store_state(ref, t):
        t = t.reshape(ref.shape[0], ref.shape[-2], ref.shape[-1])
        if state == "first":
            for d in range(ref.shape[1]):
                ref[:, d] = t
        else:
            ref[...] = t

    h = _modulated_norm(x_ref[...], g_ref[...], mod_ref, 3).astype(BF16)
    h_ref[...] = h

    def head_norm(t, gain):
        sq = (t * t).astype(BF16)
        width = seg_ref.shape[0]
        mean_sq = jnp.concatenate(
            [_dot(sq[:, width * j:width * (j + 1)], seg_ref[...]) for j in range(QK_WIDTH // width)],
            axis=1)
        return t * lax.rsqrt(mean_sq + EPS) * gain

    def head_chunks(t):
        return [t[:, LANES * i:LANES * (i + 1)] for i in range(N_HEADS)]

    def rotate(chunk):
        return (chunk * cos_ref[...]
                + pltpu.roll(chunk, LANES - HEAD_DIM // 4, axis=1) * sin_first_ref[...]
                + pltpu.roll(chunk, HEAD_DIM // 4, axis=1) * sin_second_ref[...])

    q = head_norm(_dot(h, win_ref[:, 0:QK_WIDTH]), gq_ref[...])
    for i, chunk in enumerate(head_chunks(q)):
        if rope:
            chunk = rotate(chunk)
        q_ref[:, LANES * i:LANES * (i + 1)] = (chunk * (HEAD_DIM ** -0.5 * LOG2_E)).astype(BF16)

    k = head_norm(_dot(h, win_ref[:, QK_WIDTH:2 * QK_WIDTH]), gk_ref[...])
    if state:
        store_state(sk_ref, k)
    for i, chunk in enumerate(head_chunks(k)):
        if rope:
            chunk = rotate(chunk)
        k_ref[:, LANES * i:LANES * (i + 1)] = chunk.astype(BF16)

    v0 = 2 * QK_WIDTH
    v = _dot(h, win_ref[:, v0:v0 + ATTN_WIDTH])
    if state:
        store_state(sv_ref, v)
    v_ref[...] = v.astype(BF16)

    p0 = v0 + ATTN_WIDTH
    up_ref[...] = _dot(h, win_ref[:, p0:p0 + POOL_WIDTH])
    f0 = p0 + POOL_WIDTH
    uf_ref[...] = _dot(h, win_ref[:, f0:f0 + FOURIER_WIDTH]).astype(BF16)


def _mixer_in(x, p, layer, mod_row, rope_tables, *, seq, prev_state):
    n = x.shape[0]
    rope = rope_tables is not None
    tiles_per_seq = seq // TOKEN_TILE
    tok = lambda width: pl.BlockSpec((TOKEN_TILE, width), lambda i: (i, 0))
    in_specs = [
        tok(D_MODEL),
        _mod_spec(layer, mod_row(TOKEN_TILE)),
        _per_layer(p["norm_g"], layer, 1),
        _per_layer(p["w_in"], layer, resident=True),
        _resident(p["seg"].shape, lambda i: (0, 0)),
        _per_layer(p["q_gain"], layer),
        _per_layer(p["k_gain"], layer),
    ]
    args = [x, p["mod"], p["norm_g"], p["w_in"], p["seg"], p["q_gain"], p["k_gain"]]
    if rope:
        in_specs += [pl.BlockSpec((TOKEN_TILE, LANES), lambda i: (i % tiles_per_seq, 0))] * 3
        args += list(rope_tables)
    out_shape = [jax.ShapeDtypeStruct((n, D_MODEL), BF16),
                 jax.ShapeDtypeStruct((n, QK_WIDTH), BF16),
                 jax.ShapeDtypeStruct((n, QK_WIDTH), BF16),
                 jax.ShapeDtypeStruct((n, ATTN_WIDTH), BF16),
                 jax.ShapeDtypeStruct((n, POOL_WIDTH), F32),
                 jax.ShapeDtypeStruct((n, FOURIER_WIDTH), BF16)]
    out_specs = [tok(D_MODEL), tok(QK_WIDTH), tok(QK_WIDTH), tok(ATTN_WIDTH),
                 tok(POOL_WIDTH), tok(FOURIER_WIDTH)]
    state, aliases = None, {}
    if prev_state is not None:
        depth = p["w_in"].shape[0]
        per_tile = TOKEN_TILE // seq
        assert per_tile * seq == TOKEN_TILE
        out_shape += [jax.ShapeDtypeStruct((n // seq, depth, seq, width), F32)
                      for width in (QK_WIDTH, ATTN_WIDTH)]
        if prev_state:
            state = "update"
            slot = lambda width: pl.BlockSpec((per_tile, None, seq, width), lambda i: (i, layer, 0, 0))
            aliases = {len(args): len(out_specs), len(args) + 1: len(out_specs) + 1}
            in_specs += [pl.BlockSpec(memory_space=pl.ANY)] * 2
            args += list(prev_state)
        else:
            state = "first"
            slot = lambda width: pl.BlockSpec((per_tile, depth, seq, width), lambda i: (i, 0, 0, 0))
        out_specs += [slot(QK_WIDTH), slot(ATTN_WIDTH)]
    return pl.pallas_call(
        functools.partial(_mixer_in_kernel, rope=rope, state=state),
        out_shape=out_shape,
        grid=(n // TOKEN_TILE,),
        in_specs=in_specs,
        out_specs=out_specs,
        input_output_aliases=aliases,
        compiler_params=_params(1),
        name="mixer_in",
    )(*args)


def _attention_kernel(*refs, n_past, lam_init, heads):
    refs = list(refs)
    q_ref, k_ref, v_ref = refs[:3]
    refs = refs[3:]
    if n_past:
        ck_ref, cv_ref = refs[:2]
        refs = refs[2:]
    lam_ref, gain_ref, o_ref = refs[:3]
    head_lanes = lambda j: slice(LANES * j, LANES * (j + 1))

    n_q = q_ref.shape[0]
    n_keys = n_past + k_ref.shape[0]
    key_chunk = KEY_CHUNK
    past_chunks = n_past // key_chunk

    if n_past:
        k_past, v_past = refs[3:]
        group = pl.program_id(1)

        @pl.when((pl.program_id(2) == 0) & (group == 0))
        def _():
            for h in range(N_HEADS):
                k_past[h] = ck_ref[pl.ds(h, n_past, stride=N_HEADS), :].astype(BF16)
                v_past[h] = cv_ref[pl.ds(h, n_past, stride=N_HEADS), :].astype(BF16)

    def chunk_of(past, new, j, c):
        if c < past_chunks:
            return past[group * heads + j, key_chunk * c:key_chunk * (c + 1), :]
        c -= past_chunks
        return new[key_chunk * c:key_chunk * (c + 1), head_lanes(j)]

    keys_of = lambda j, c: chunk_of(k_past if n_past else None, k_ref, j, c)
    values_of = lambda j, c: chunk_of(v_past if n_past else None, v_ref, j, c)

    lq = lam_ref[...]
    lam = (jnp.exp(jnp.sum(lq[0:1, :] * lq[1:2, :], axis=-1, keepdims=True))
           - jnp.exp(jnp.sum(lq[2:3, :] * lq[3:4, :], axis=-1, keepdims=True)) + lam_init)

    lane = lax.broadcasted_iota(jnp.int32, (n_q, LANES), 1)
    zero = jnp.zeros((n_q, LANES), BF16)

    for j in range(heads):
        q = q_ref[:, head_lanes(j)]
        q_both = jnp.concatenate(
            [jnp.where(lane < HEAD_DIM, q, zero), jnp.where(lane >= HEAD_DIM, q, zero)], axis=0)

        run_max = acc = den = None
        for c in range(n_keys // key_chunk):
            s = _dot_nt(q_both, keys_of(j, c))
            new_max = jnp.max(s, axis=-1, keepdims=True)
            if c:
                new_max = jnp.maximum(run_max, new_max)
            e = jnp.exp2(s - new_max)
            pv = _dot(e.astype(BF16), values_of(j, c))
            part = e[:, 0:LANES]
            for t in range(1, key_chunk // LANES):
                part = part + e[:, LANES * t:LANES * (t + 1)]
            if c == 0:
                acc, den = pv, part
            else:
                alpha = jnp.exp2(run_max - new_max)
                acc, den = acc * alpha + pv, den * alpha + part
            run_max = new_max
        o_both = acc / jnp.sum(den, axis=-1, keepdims=True)

        o = o_both[0:n_q] - lam * o_both[n_q:]
        o = o * lax.rsqrt(jnp.mean(o * o, axis=-1, keepdims=True) + EPS) * gain_ref[...]
        o_ref[:, head_lanes(j)] = (o * (1.0 - lam_init)).astype(BF16)


def _attention(q, k, v, cache_k, cache_v, p, layer):
    batch, seq, _ = q.shape
    lam_init = 0.8 - 0.6 * math.exp(-0.3 * layer)
    n_past = 0 if cache_k is None else cache_k.shape[2] // N_HEADS
    n_keys = n_past + seq
    assert seq % KEY_CHUNK == 0 and n_past % KEY_CHUNK == 0
    q_tile = min(Q_TILE, seq)
    heads = max(1, min(N_HEADS, ATTN_KEYS_PER_STEP // n_keys))
    width = LANES * heads
    seq_block = pl.BlockSpec((None, seq, width), lambda b, g, i: (b, 0, g))
    in_specs = [pl.BlockSpec((None, q_tile, width), lambda b, g, i: (b, i, g)), seq_block, seq_block]
    args = [q, k, v]
    scratch = []
    if n_past:
        past_block = pl.BlockSpec((None, None, n_past * N_HEADS, LANES), lambda b, g, i: (b, layer, 0, 0))
        in_specs += [past_block, past_block]
        args += [cache_k, cache_v]
        scratch = [pltpu.VMEM((N_HEADS, n_past, LANES), BF16), pltpu.VMEM((N_HEADS, n_past, V_DIM), BF16)]
    in_specs += [_per_layer(p["lam_qk"], layer), _per_layer(p["subln_g"], layer)]
    args += [p["lam_qk"], p["subln_g"]]
    return pl.pallas_call(
        functools.partial(_attention_kernel, n_past=n_past, lam_init=lam_init, heads=heads),
        out_shape=jax.ShapeDtypeStruct((batch, seq, ATTN_WIDTH), BF16),
        grid=(batch, N_HEADS // heads, seq // q_tile),
        in_specs=in_specs,
        out_specs=pl.BlockSpec((None, q_tile, width), lambda b, g, i: (b, i, g)),
        scratch_shapes=scratch,
        compiler_params=_params(3),
        name="diff_attention",
    )(*args)


def _pool_kernel(u_ref, w_ref, scale_ref, o_ref):
    seq = u_ref.shape[0]
    row = lax.broadcasted_iota(jnp.int32, (seq, POOL_GROUP), 0)

    def shifted(x, k):
        keep = (row >= k) if k > 0 else (row < seq + k)
        return jnp.where(keep, pltpu.roll(x, k % seq, axis=0), 0.0)

    for g, window in enumerate(POOL_WINDOWS):
        half = window // 2
        lanes = slice(POOL_GROUP * g, POOL_GROUP * (g + 1))
        u = u_ref[:, lanes]
        ahead = behind = u
        n = 1
        while n < half:
            ahead = ahead + shifted(ahead, -n)
            behind = behind + shifted(behind, n)
            n *= 2
        win_sum = ahead + shifted(behind, 1)
        count = (jnp.minimum(row + half, seq) - jnp.maximum(row - half, 0)).astype(F32)
        d = win_sum / count - u
        o_ref[:, lanes] = (_dot(d.astype(BF16), w_ref[g]) * scale_ref[:, lanes]).astype(BF16)


def _fft_across_blocks(blocks):
    n = len(blocks)
    if n == 1:
        return blocks
    even = _fft_across_blocks(blocks[0::2])
    odd = _fft_across_blocks(blocks[1::2])
    out = [None] * n
    for k in range(n // 2):
        o_re, o_im = odd[k]
        if k == 0:
            t_re, t_im = o_re, o_im
        elif 4 * k == n:
            t_re, t_im = o_im, -o_re
        else:
            c, s = math.cos(2 * math.pi * k / n), -math.sin(2 * math.pi * k / n)
            t_re, t_im = o_re * c - o_im * s, o_re * s + o_im * c
        e_re, e_im = even[k]
        out[k] = (e_re + t_re, e_im + t_im)
        out[k + n // 2] = (e_re - t_re, e_im - t_im)
    return out


def _fourier_kernel(*refs, radix):
    if radix > 1:
        x_ref, block_ref, chan_ref, tw_cos_ref, tw_sin_ref, o_ref, x32, w_scr = refs
        for g in range(FOURIER_GROUPS):
            x32[g] = x_ref[:, FOURIER_GROUP * g:FOURIER_GROUP * (g + 1)].astype(F32)
    else:
        x_ref, block_ref, chan_ref, o_ref, w_scr = refs
    seq = x_ref.shape[0]
    m = seq // radix

    blocks = []
    for j in range(radix):
        if radix > 1:
            x_j = jnp.concatenate([x32[g, pl.ds(j, m, stride=radix), :] for g in range(FOURIER_GROUPS)],
                                  axis=1).astype(BF16)
        else:
            x_j = x_ref[...]
        g = _dot(block_ref[...], x_j)
        g_re, g_im = g[0:m], g[m:2 * m]
        if j:
            c = jnp.concatenate([tw_cos_ref[j]] * FOURIER_GROUPS, axis=1)
            s = jnp.concatenate([tw_sin_ref[j]] * FOURIER_GROUPS, axis=1)
            g_re, g_im = g_re * c + g_im * s, g_im * c - g_re * s
        blocks.append((g_re, g_im))

    for k1, (w_re, w_im) in enumerate(_fft_across_blocks(blocks)):
        rows = slice(m * k1, m * (k1 + 1))
        w_scr[rows, 0:FOURIER_WIDTH] = w_re.astype(BF16)
        w_scr[rows, FOURIER_WIDTH:] = w_im.astype(BF16)

    scale = 1.0 / math.sqrt(seq * FOURIER_GROUP)
    for g in range(FOURIER_GROUPS):
        lanes = slice(FOURIER_GROUP * g, FOURIER_GROUP * (g + 1))
        im_lanes = slice(FOURIER_WIDTH + FOURIER_GROUP * g, FOURIER_WIDTH + FOURIER_GROUP * (g + 1))
        w = jnp.concatenate([w_scr[:, lanes], w_scr[:, im_lanes]], axis=1)
        o_ref[:, lanes] = (_dot(w, chan_ref[...]) * scale).astype(BF16)


def _sequence_mix_kernel(u_ref, w_pool_ref, scale_ref, *refs, radix):
    n_fourier_in = 5 if radix > 1 else 3
    pool_out_ref = refs[n_fourier_in]
    _pool_kernel(u_ref, w_pool_ref, scale_ref, pool_out_ref)
    _fourier_kernel(*refs[:n_fourier_in], *refs[n_fourier_in + 1:], radix=radix)


def _sequence_mix(u, x, p, layer, tables):
    batch, seq, _ = x.shape
    radix = seq // DFT_BLOCK
    assert radix * DFT_BLOCK == seq and radix & (radix - 1) == 0
    block, chan, tw_cos, tw_sin = tables
    whole = lambda t: pl.BlockSpec(t.shape, lambda b: (0,) * t.ndim)
    per_seq = lambda width: pl.BlockSpec((None, seq, width), lambda b: (b, 0, 0))
    in_specs = [per_seq(POOL_WIDTH), _per_layer(p["w_pool"], layer), _per_layer(p["pool_scale"], layer),
                per_seq(FOURIER_WIDTH), whole(block), whole(chan)]
    args = [u, p["w_pool"], p["pool_scale"], x, block, chan]
    scratch = [pltpu.VMEM((seq, 2 * FOURIER_WIDTH), BF16)]
    if radix > 1:
        in_specs += [whole(tw_cos), whole(tw_sin)]
        args += [tw_cos, tw_sin]
        scratch = [pltpu.VMEM((FOURIER_GROUPS, seq, FOURIER_GROUP), F32)] + scratch
    return pl.pallas_call(
        functools.partial(_sequence_mix_kernel, radix=radix),
        out_shape=[jax.ShapeDtypeStruct((batch, seq, POOL_WIDTH), BF16),
                   jax.ShapeDtypeStruct((batch, seq, FOURIER_WIDTH), BF16)],
        grid=(batch,),
        in_specs=in_specs,
        out_specs=[per_seq(POOL_WIDTH), per_seq(FOURIER_WIDTH)],
        scratch_shapes=scratch,
        compiler_params=_params(1),
        name="sequence_mix",
    )(*args)


def _mixer_out_kernel(x_ref, mod_ref, h_ref, a_ref, p_ref, f_ref,
                      wg_ref, wpa_ref, wpp_ref, wpf_ref, wo_ref, g_ref, w13_ref, w2_ref,
                      o_ref, h_scr, acc_scr):
    h = h_ref[...]

    def gated(branch_ref, w_ref, index):
        gate = jax.nn.sigmoid(_dot(h, wg_ref[:, D_MODEL * index:D_MODEL * (index + 1)]))
        return gate * _dot(branch_ref[...], w_ref[...])

    m = gated(a_ref, wpa_ref, 0) + gated(p_ref, wpp_ref, 1) + gated(f_ref, wpf_ref, 2)
    x = x_ref[...] + mod_ref[5:6, :] * _dot(m.astype(BF16), wo_ref[...])
    o_ref[...] = _ffn_block(x, mod_ref, g_ref, w13_ref, w2_ref, h_scr, acc_scr, 6)


def _mixer_out_ffn(x, p, layer, mod_row, h, attn, pooled, fourier):
    n = x.shape[0]
    tok = lambda width: pl.BlockSpec((TOKEN_TILE, width), lambda i: (i, 0))
    names = ["w_gate", "w_pa", "w_pp", "w_pf", "w_out"]
    return pl.pallas_call(
        _mixer_out_kernel,
        out_shape=jax.ShapeDtypeStruct((n, D_MODEL), F32),
        grid=(n // TOKEN_TILE,),
        in_specs=[tok(D_MODEL), _mod_spec(layer, mod_row(TOKEN_TILE)),
                  tok(D_MODEL), tok(ATTN_WIDTH), tok(POOL_WIDTH), tok(FOURIER_WIDTH)]
                 + [_per_layer(p[name], layer, resident=True) for name in names]
                 + [_per_layer(p["norm_g"], layer, 2),
                    _per_layer(p["w13"], layer, 1, resident=True),
                    _per_layer(p["w2"], layer, 1, resident=True)],
        out_specs=tok(D_MODEL),
        scratch_shapes=[pltpu.VMEM((TOKEN_TILE, D_MODEL), BF16), pltpu.VMEM((TOKEN_TILE, D_MODEL), F32)],
        compiler_params=_params(1),
        name="mixer_out_ffn",
    )(x, p["mod"], h, attn, pooled, fourier, *[p[name] for name in names],
      p["norm_g"], p["w13"], p["w2"])


def _trunk_layer(x, batch, seq, p, layer, mod_row, cache, prev_state, rope_tables, dft_tables):
    x = _ffn(x, p, layer, mod_row)

    outs = _mixer_in(x, p, layer, mod_row, rope_tables, seq=seq, prev_state=prev_state)
    h, q, k, v, up, uf = outs[:6]
    per_seq = lambda t: t.reshape(batch, seq, t.shape[-1])
    cache_k, cache_v = (None, None) if cache is None else cache
    attn = _attention(per_seq(q), per_seq(k), per_seq(v), cache_k, cache_v, p, layer)
    pooled, fourier = _sequence_mix(per_seq(up), per_seq(uf), p, layer, dft_tables)
    flat = lambda t: t.reshape(batch * seq, t.shape[-1])
    x = _mixer_out_ffn(x, p, layer, mod_row, h, flat(attn), flat(pooled), flat(fourier))
    return x, tuple(outs[6:])


def kernel(x_prompt, x_sample, cache_k, cache_v, c, c_ctx, w_ada, b_ada, norm_g, ffn_w13, ffn_w2, w_in, q_norm_g, k_norm_g, lam_qk, subln_g, w_pool, pool_scale, w_gate, w_pa, w_pp, w_pf, w_out):
    batch, seq, _ = x_prompt.shape
    dec_batch, dec_seq, _ = x_sample.shape
    depth = w_ada.shape[0]
    n_past = cache_k.shape[2]
    assert dec_batch + 1 <= MOD_ROWS
    for tile in (TOKEN_TILE, FFN_TOKEN_TILE):
        assert dec_seq % tile == 0 and (batch * seq) % tile == 0

    cond = jnp.concatenate(
        [c, c_ctx[None], jnp.zeros((MOD_ROWS - dec_batch - 1, D_MODEL), F32)], axis=0)

    cast = lambda t: t.astype(BF16)
    rows = lambda t: t.reshape(t.shape[:-1] + (1, t.shape[-1]))
    tile_gain = lambda g: rows(jnp.tile(g, (1, QK_WIDTH // HEAD_DIM)))
    p = {
        "mod": _modulation(cond, w_ada, b_ada),
        "norm_g": rows(norm_g), "w13": cast(ffn_w13), "w2": cast(ffn_w2), "w_in": cast(w_in),
        "seg": cast(jnp.asarray(_segment_mean_matrix())),
        "q_gain": tile_gain(q_norm_g), "k_gain": tile_gain(k_norm_g),
        "lam_qk": lam_qk, "subln_g": rows(subln_g),
        "w_pool": cast(w_pool), "pool_scale": rows(pool_scale),
        "w_gate": cast(w_gate), "w_pa": cast(w_pa), "w_pp": cast(w_pp), "w_pf": cast(w_pf),
        "w_out": cast(w_out),
    }
    def dft_tables(n):
        block, chan, tw_cos, tw_sin = (jnp.asarray(t) for t in _dft_tables(n))
        return cast(block), cast(chan), tw_cos, tw_sin
    dft_ctx, dft_lat = dft_tables(seq), dft_tables(dec_seq)
    rope_lat = tuple(jnp.asarray(t) for t in _rope_tables(dec_seq))
    cache = (cache_k.reshape(dec_batch, depth, n_past * N_HEADS, 2 * HEAD_DIM),
             cache_v.reshape(dec_batch, depth, n_past * N_HEADS, V_DIM))

    ctx_row = lambda tile: (lambda i: dec_batch)
    lat_row = lambda tile: (lambda i: (i * tile) // dec_seq)

    xp = x_prompt.reshape(batch * seq, D_MODEL)
    xs = x_sample.reshape(dec_batch * dec_seq, D_MODEL)
    state = ()
    for l in range(depth):
        xp, state = _trunk_layer(xp, batch, seq, p, l, ctx_row, None, state, None, dft_ctx)
        xs, _ = _trunk_layer(xs, dec_batch, dec_seq, p, l, lat_row, cache, None, rope_lat, dft_lat)

    state_k, state_v = state
    return (xp.reshape(batch, seq, D_MODEL), xs.reshape(dec_batch, dec_seq, D_MODEL),
            state_k.reshape(batch, depth, seq, N_HEADS, 2 * HEAD_DIM),
            state_v.reshape(batch, depth, seq, N_HEADS, V_DIM))
```

```python
import functools
import math

import numpy as np
import jax
import jax.numpy as jnp
from jax import lax
from jax.experimental import pallas as pl
from jax.experimental.pallas import tpu as pltpu

D_MODEL = 1024
N_HEADS = 8
HEAD_DIM = 64
V_DIM = 2 * HEAD_DIM
QK_WIDTH = N_HEADS * 2 * HEAD_DIM
ATTN_WIDTH = N_HEADS * V_DIM
POOL_WINDOWS = (2, 4, 8, 16)
POOL_GROUP = 128
POOL_WIDTH = POOL_GROUP * len(POOL_WINDOWS)
FOURIER_GROUPS = 4
FOURIER_GROUP = 128
FOURIER_WIDTH = FOURIER_GROUPS * FOURIER_GROUP
D_FF = 2816
N_MOD = 9
GRID_W = 64
ROPE_BASE = 10000.0
EPS = 1e-6

LANES = 128
MXU_WIDTH = 256
V7X_VMEM_BYTES = 64 * 1024 * 1024
PIPELINE_BUFFERS = 2

MOD_ROWS = 16
MOD_COL_TILE = 2304
FF_CHUNK = 256
TOKEN_TILE = 512
FFN_TOKEN_TILE = 1024
Q_TILE = 2048
KEY_CHUNK = 256
ATTN_KEYS_PER_STEP = 4096
LOG2_E = math.log2(math.e)
DFT_BLOCK = MXU_WIDTH

F32 = jnp.float32
BF16 = jnp.bfloat16


def _dot(a, b):
    return jnp.dot(a, b, preferred_element_type=F32)


def _dot_nt(a, b):
    return lax.dot_general(a, b, (((1,), (1,)), ((), ())), preferred_element_type=F32)


def _resident(block_shape, index_map):
    return pl.BlockSpec(block_shape, index_map, pipeline_mode=pl.Buffered(1))


def _per_layer(array, *lead, resident=False):
    tail = array.shape[len(lead):]
    index = tuple(lead) + (0,) * len(tail)
    make = _resident if resident else pl.BlockSpec
    return make((None,) * len(lead) + tail, lambda *_: index)


def _mod_spec(layer, mod_row):
    return pl.BlockSpec((None, None, N_MOD, D_MODEL), lambda i: (layer, mod_row(i), 0, 0))


def _window_bytes(spec, dtype):
    if spec.block_shape is None:
        return 0
    single = spec.pipeline_mode is not None and spec.pipeline_mode.buffer_count == 1
    elements = math.prod(d for d in spec.block_shape if d is not None)
    return elements * jnp.dtype(dtype).itemsize * (1 if single else PIPELINE_BUFFERS)


def _call(body, *, name, grid, in_specs, args, out_specs, out_shape, live_bytes, scratch=(), aliases=None):
    outs, out_list = (out_shape, out_specs) if isinstance(out_shape, (list, tuple)) else ([out_shape], [out_specs])
    planned = (sum(_window_bytes(s, a.dtype) for s, a in zip(in_specs, args))
               + sum(_window_bytes(s, o.dtype) for s, o in zip(out_list, outs))
               + sum(math.prod(s.shape) * jnp.dtype(s.dtype).itemsize for s in scratch)
               + live_bytes)
    assert planned <= V7X_VMEM_BYTES, (name, planned)
    return pl.pallas_call(
        body,
        out_shape=out_shape,
        grid=grid,
        in_specs=in_specs,
        out_specs=out_specs,
        scratch_shapes=list(scratch),
        input_output_aliases=aliases or {},
        compiler_params=pltpu.CompilerParams(
            dimension_semantics=("arbitrary",) * len(grid), vmem_limit_bytes=planned),
        name=name,
    )(*args)


def _modulated_norm(x, gain, mod_ref, first):
    y = x * lax.rsqrt(jnp.mean(x * x, axis=-1, keepdims=True) + EPS) * gain
    return y * (1.0 + mod_ref[first + 1:first + 2, :]) + mod_ref[first:first + 1, :]


@functools.lru_cache(maxsize=None)
def _rope_tables(seq):
    n_freq = HEAD_DIM // 4
    lane = np.arange(LANES)
    within = lane % HEAD_DIM
    axis = within // (2 * n_freq)
    half = (within % (2 * n_freq)) // n_freq
    freq = lane % n_freq
    inv = ROPE_BASE ** (-np.arange(n_freq, dtype=np.float64) / n_freq)
    pos = np.arange(seq)
    coord = np.where(axis[None, :] == 0, (pos // GRID_W)[:, None], (pos % GRID_W)[:, None])
    ang = coord.astype(np.float64) * inv[freq][None, :]
    cos, sin = np.cos(ang), np.sin(ang)
    sin_first = np.where(half[None, :] == 0, -sin, 0.0)
    sin_second = np.where(half[None, :] == 1, sin, 0.0)
    return tuple(np.asarray(t, np.float32) for t in (cos, sin_first, sin_second))


@functools.lru_cache(maxsize=None)
def _segment_mean_matrix():
    idx = np.arange(MXU_WIDTH) // HEAD_DIM
    return np.asarray((idx[:, None] == idx[None, :]) / HEAD_DIM, np.float32)


@functools.lru_cache(maxsize=None)
def _dft_tables(seq):
    def cs(n, rows, cols):
        ang = 2.0 * np.pi * ((rows[:, None] * cols[None, :]) % n).astype(np.float64) / n
        return np.cos(ang), np.sin(ang)
    k_block = np.arange(DFT_BLOCK, dtype=np.int64)
    k_chan = np.arange(FOURIER_GROUP, dtype=np.int64)
    c_block, s_block = cs(DFT_BLOCK, k_block, k_block)
    c_chan, s_chan = cs(FOURIER_GROUP, k_chan, k_chan)
    c_tw, s_tw = cs(seq, np.arange(seq // DFT_BLOCK, dtype=np.int64), k_block)
    lanes = lambda t: np.repeat(t[:, :, None], LANES, axis=2)
    tables = (np.concatenate([c_block, -s_block], axis=0), np.concatenate([c_chan, s_chan], axis=0),
              lanes(c_tw), lanes(s_tw))
    return tuple(np.asarray(t, np.float32) for t in tables)


def _mod_kernel(c_ref, w_ref, b_ref, o_ref):
    c = c_ref[...]
    s = c * jax.nn.sigmoid(c)
    s_hi = s.astype(BF16)
    s_lo = (s - s_hi.astype(F32)).astype(BF16)
    w = w_ref[...]
    w_hi = w.astype(BF16)
    w_lo = (w - w_hi.astype(F32)).astype(BF16)
    o_ref[...] = _dot(s_hi, w_hi) + _dot(s_lo, w_hi) + _dot(s_hi, w_lo) + b_ref[...]


def _modulation(cond, w_ada, b_ada):
    depth = w_ada.shape[0]
    width = N_MOD * D_MODEL
    out = _call(
        _mod_kernel,
        name="adaln_modulation",
        grid=(depth, width // MOD_COL_TILE),
        in_specs=[
            pl.BlockSpec((MOD_ROWS, D_MODEL), lambda l, n: (0, 0)),
            pl.BlockSpec((None, D_MODEL, MOD_COL_TILE), lambda l, n: (l, 0, n)),
            pl.BlockSpec((None, 1, MOD_COL_TILE), lambda l, n: (l, 0, n)),
        ],
        args=(cond, w_ada, b_ada.reshape(depth, 1, width)),
        out_specs=pl.BlockSpec((None, MOD_ROWS, MOD_COL_TILE), lambda l, n: (l, 0, n)),
        out_shape=jax.ShapeDtypeStruct((depth, MOD_ROWS, width), F32),
        live_bytes=D_MODEL * MOD_COL_TILE * (4 + 4 + 2 + 2),
    )
    return out.reshape(depth, MOD_ROWS, N_MOD, D_MODEL)


def _ffn_block(x, mod_ref, g_ref, w13_ref, w2_ref, h_scr, acc_scr, mod_first):
    h_scr[...] = _modulated_norm(x, g_ref[...], mod_ref, mod_first).astype(BF16)
    acc_scr[...] = jnp.zeros_like(acc_scr)

    for j in range(D_FF // FF_CHUNK):
        cols = slice(FF_CHUNK * j, FF_CHUNK * (j + 1))
        h = h_scr[...]
        gate = _dot(h, w13_ref[:, cols])
        up = _dot(h, w13_ref[:, D_FF + FF_CHUNK * j:D_FF + FF_CHUNK * (j + 1)])
        act = (gate * jax.nn.sigmoid(gate)) * up
        acc_scr[...] += _dot(act.astype(BF16), w2_ref[cols, :])

    return x + (0.5 * mod_ref[mod_first + 2:mod_first + 3, :]) * acc_scr[...]


def _ffn_kernel(x_ref, mod_ref, g_ref, w13_ref, w2_ref, o_ref, h_scr, acc_scr):
    o_ref[...] = _ffn_block(x_ref[...], mod_ref, g_ref, w13_ref, w2_ref, h_scr, acc_scr, 0)


def _ffn_live_bytes(tile):
    return tile * (2 * D_MODEL * 4 + FF_CHUNK * (3 * 4 + 2))


def _ffn(x, p, layer, mod_row):
    n = x.shape[0]
    half = 0
    tile = FFN_TOKEN_TILE
    return _call(
        _ffn_kernel,
        name="swiglu_ffn",
        grid=(n // tile,),
        in_specs=[
            pl.BlockSpec((tile, D_MODEL), lambda i: (i, 0)),
            _mod_spec(layer, mod_row(tile)),
            _per_layer(p["norm_g"], layer, 2 * half),
            _per_layer(p["w13"], layer, half, resident=True),
            _per_layer(p["w2"], layer, half, resident=True),
        ],
        args=(x, p["mod"], p["norm_g"], p["w13"], p["w2"]),
        out_specs=pl.BlockSpec((tile, D_MODEL), lambda i: (i, 0)),
        out_shape=jax.ShapeDtypeStruct((n, D_MODEL), F32),
        scratch=[pltpu.VMEM((tile, D_MODEL), BF16), pltpu.VMEM((tile, D_MODEL), F32)],
        live_bytes=_ffn_live_bytes(tile),
    )


def _mixer_in_kernel(*refs, rope, state):
    refs = list(refs)
    x_ref, mod_ref, g_ref, win_ref, seg_ref, gq_ref, gk_ref = refs[:7]
    refs = refs[7:]
    if rope:
        cos_ref, sin_first_ref, sin_second_ref = refs[:3]
        refs = refs[3:]
    if state == "update":
        refs = refs[2:]
    h_ref, q_ref, k_ref, v_ref, up_ref, uf_ref = refs[:6]
    refs = refs[6:]
    if state:
        sk_ref, sv_ref = refs

    def store_state(ref, t):
        t = t.reshape(ref.shape[0], ref.shape[-2], ref.shape[-1])
        if state == "first":
            for d in range(ref.shape[1]):
                ref[:, d] = t
        else:
            ref[...] = t

    h = _modulated_norm(x_ref[...], g_ref[...], mod_ref, 3).astype(BF16)
    h_ref[...] = h

    def head_norm(t, gain):
        sq = (t * t).astype(BF16)
        width = seg_ref.shape[0]
        mean_sq = jnp.concatenate(
            [_dot(sq[:, width * j:width * (j + 1)], seg_ref[...]) for j in range(QK_WIDTH // width)],
            axis=1)
        return t * lax.rsqrt(mean_sq + EPS) * gain

    def head_chunks(t):
        return [t[:, LANES * i:LANES * (i + 1)] for i in range(N_HEADS)]

    def rotate(chunk):
        return (chunk * cos_ref[...]
                + pltpu.roll(chunk, LANES - HEAD_DIM // 4, axis=1) * sin_first_ref[...]
                + pltpu.roll(chunk, HEAD_DIM // 4, axis=1) * sin_second_ref[...])

    q = head_norm(_dot(h, win_ref[:, 0:QK_WIDTH]), gq_ref[...])
    for i, chunk in enumerate(head_chunks(q)):
        if rope:
            chunk = rotate(chunk)
        q_ref[:, LANES * i:LANES * (i + 1)] = (chunk * (HEAD_DIM ** -0.5 * LOG2_E)).astype(BF16)

    k = head_norm(_dot(h, win_ref[:, QK_WIDTH:2 * QK_WIDTH]), gk_ref[...])
    if state:
        store_state(sk_ref, k)
    for i, chunk in enumerate(head_chunks(k)):
        if rope:
            chunk = rotate(chunk)
        k_ref[:, LANES * i:LANES * (i + 1)] = chunk.astype(BF16)

    v0 = 2 * QK_WIDTH
    v = _dot(h, win_ref[:, v0:v0 + ATTN_WIDTH])
    if state:
        store_state(sv_ref, v)
    v_ref[...] = v.astype(BF16)

    p0 = v0 + ATTN_WIDTH
    up_ref[...] = _dot(h, win_ref[:, p0:p0 + POOL_WIDTH])
    f0 = p0 + POOL_WIDTH
    uf_ref[...] = _dot(h, win_ref[:, f0:f0 + FOURIER_WIDTH]).astype(BF16)


def _mixer_in(x, p, layer, mod_row, rope_tables, *, seq, prev_state):
    n = x.shape[0]
    rope = rope_tables is not None
    tiles_per_seq = seq // TOKEN_TILE
    tok = lambda width: pl.BlockSpec((TOKEN_TILE, width), lambda i: (i, 0))
    in_specs = [
        tok(D_MODEL),
        _mod_spec(layer, mod_row(TOKEN_TILE)),
        _per_layer(p["norm_g"], layer, 1),
        _per_layer(p["w_in"], layer, resident=True),
        _resident(p["seg"].shape, lambda i: (0, 0)),
        _per_layer(p["q_gain"], layer),
        _per_layer(p["k_gain"], layer),
    ]
    args = [x, p["mod"], p["norm_g"], p["w_in"], p["seg"], p["q_gain"], p["k_gain"]]
    if rope:
        in_specs += [pl.BlockSpec((TOKEN_TILE, LANES), lambda i: (i % tiles_per_seq, 0))] * 3
        args += list(rope_tables)
    out_shape = [jax.ShapeDtypeStruct((n, D_MODEL), BF16),
                 jax.ShapeDtypeStruct((n, QK_WIDTH), BF16),
                 jax.ShapeDtypeStruct((n, QK_WIDTH), BF16),
                 jax.ShapeDtypeStruct((n, ATTN_WIDTH), BF16),
                 jax.ShapeDtypeStruct((n, POOL_WIDTH), F32),
                 jax.ShapeDtypeStruct((n, FOURIER_WIDTH), BF16)]
    out_specs = [tok(D_MODEL), tok(QK_WIDTH), tok(QK_WIDTH), tok(ATTN_WIDTH),
                 tok(POOL_WIDTH), tok(FOURIER_WIDTH)]
    state, aliases = None, {}
    if prev_state is not None:
        depth = p["w_in"].shape[0]
        per_tile = TOKEN_TILE // seq
        assert per_tile * seq == TOKEN_TILE
        out_shape += [jax.ShapeDtypeStruct((n // seq, depth, seq, width), F32)
                      for width in (QK_WIDTH, ATTN_WIDTH)]
        if prev_state:
            state = "update"
            slot = lambda width: pl.BlockSpec((per_tile, None, seq, width), lambda i: (i, layer, 0, 0))
            aliases = {len(args): len(out_specs), len(args) + 1: len(out_specs) + 1}
            in_specs += [pl.BlockSpec(memory_space=pl.ANY)] * 2
            args += list(prev_state)
        else:
            state = "first"
            slot = lambda width: pl.BlockSpec((per_tile, depth, seq, width), lambda i: (i, 0, 0, 0))
        out_specs += [slot(QK_WIDTH), slot(ATTN_WIDTH)]
    return _call(
        functools.partial(_mixer_in_kernel, rope=rope, state=state),
        name="mixer_in",
        grid=(n // TOKEN_TILE,),
        in_specs=in_specs,
        args=args,
        out_specs=out_specs,
        out_shape=out_shape,
        aliases=aliases,
        live_bytes=TOKEN_TILE * D_MODEL * (4 + 2 + 4 + 2 + 4 + 4),
    )


def _attention_kernel(*refs, n_past, lam_init, heads):
    refs = list(refs)
    q_ref, k_ref, v_ref = refs[:3]
    refs = refs[3:]
    if n_past:
        ck_ref, cv_ref = refs[:2]
        refs = refs[2:]
    lam_ref, gain_ref, o_ref = refs[:3]
    head_lanes = lambda j: slice(LANES * j, LANES * (j + 1))

    n_q = q_ref.shape[0]
    n_keys = n_past + k_ref.shape[0]
    key_chunk = min(KEY_CHUNK, k_ref.shape[0])
    past_chunks = n_past // key_chunk

    if n_past:
        k_past, v_past = refs[3:]
        group = pl.program_id(1)

        @pl.when((pl.program_id(2) == 0) & (group == 0))
        def _():
            for h in range(N_HEADS):
                k_past[h] = ck_ref[pl.ds(h, n_past, stride=N_HEADS), :].astype(BF16)
                v_past[h] = cv_ref[pl.ds(h, n_past, stride=N_HEADS), :].astype(BF16)

    def chunk_of(past, new, j, c):
        if c < past_chunks:
            return past[group * heads + j, key_chunk * c:key_chunk * (c + 1), :]
        c -= past_chunks
        return new[key_chunk * c:key_chunk * (c + 1), head_lanes(j)]

    keys_of = lambda j, c: chunk_of(k_past if n_past else None, k_ref, j, c)
    values_of = lambda j, c: chunk_of(v_past if n_past else None, v_ref, j, c)

    lq = lam_ref[...]
    lam = (jnp.exp(jnp.sum(lq[0:1, :] * lq[1:2, :], axis=-1, keepdims=True))
           - jnp.exp(jnp.sum(lq[2:3, :] * lq[3:4, :], axis=-1, keepdims=True)) + lam_init)

    lane = lax.broadcasted_iota(jnp.int32, (n_q, LANES), 1)
    zero = jnp.zeros((n_q, LANES), BF16)

    for j in range(heads):
        q = q_ref[:, head_lanes(j)]
        q_both = jnp.concatenate(
            [jnp.where(lane < HEAD_DIM, q, zero), jnp.where(lane >= HEAD_DIM, q, zero)], axis=0)

        run_max = acc = den = None
        for c in range(n_keys // key_chunk):
            s = _dot_nt(q_both, keys_of(j, c))
            new_max = jnp.max(s, axis=-1, keepdims=True)
            if c:
                new_max = jnp.maximum(run_max, new_max)
            e = jnp.exp2(s - new_max)
            pv = _dot(e.astype(BF16), values_of(j, c))
            part = e[:, 0:LANES]
            for t in range(1, key_chunk // LANES):
                part = part + e[:, LANES * t:LANES * (t + 1)]
            if c == 0:
                acc, den = pv, part
            else:
                alpha = jnp.exp2(run_max - new_max)
                acc, den = acc * alpha + pv, den * alpha + part
            run_max = new_max
        o_both = acc / jnp.sum(den, axis=-1, keepdims=True)

        o = o_both[0:n_q] - lam * o_both[n_q:]
        o = o * lax.rsqrt(jnp.mean(o * o, axis=-1, keepdims=True) + EPS) * gain_ref[...]
        o_ref[:, head_lanes(j)] = (o * (1.0 - lam_init)).astype(BF16)


def _attention(q, k, v, cache_k, cache_v, p, layer):
    batch, seq, _ = q.shape
    lam_init = 0.8 - 0.6 * math.exp(-0.3 * layer)
    n_past = 0 if cache_k is None else cache_k.shape[2] // N_HEADS
    n_keys = n_past + seq
    key_chunk = min(KEY_CHUNK, seq)
    assert seq % key_chunk == 0 and n_past % key_chunk == 0
    q_tile = min(Q_TILE, seq)
    heads = max(1, min(N_HEADS, ATTN_KEYS_PER_STEP // n_keys))
    width = LANES * heads
    seq_block = pl.BlockSpec((None, seq, width), lambda b, g, i: (b, 0, g))
    in_specs = [pl.BlockSpec((None, q_tile, width), lambda b, g, i: (b, i, g)), seq_block, seq_block]
    args = [q, k, v]
    scratch = []
    if n_past:
        past_block = pl.BlockSpec((None, None, n_past * N_HEADS, LANES), lambda b, g, i: (b, layer, 0, 0))
        in_specs += [past_block, past_block]
        args += [cache_k, cache_v]
        scratch = [pltpu.VMEM((N_HEADS, n_past, LANES), BF16), pltpu.VMEM((N_HEADS, n_past, V_DIM), BF16)]
    in_specs += [_per_layer(p["lam_qk"], layer), _per_layer(p["subln_g"], layer)]
    args += [p["lam_qk"], p["subln_g"]]
    rows = 2 * q_tile
    return _call(
        functools.partial(_attention_kernel, n_past=n_past, lam_init=lam_init, heads=heads),
        name="diff_attention",
        grid=(batch, N_HEADS // heads, seq // q_tile),
        in_specs=in_specs,
        args=args,
        out_specs=pl.BlockSpec((None, q_tile, width), lambda b, g, i: (b, i, g)),
        out_shape=jax.ShapeDtypeStruct((batch, seq, ATTN_WIDTH), BF16),
        scratch=scratch,
        live_bytes=heads * rows * (2 * key_chunk * (4 + 4 + 2) + 4 * LANES * 4),
    )


def _pool_kernel(u_ref, w_ref, scale_ref, o_ref):
    seq = u_ref.shape[0]
    row = lax.broadcasted_iota(jnp.int32, (seq, POOL_GROUP), 0)

    def shifted(x, k):
        keep = (row >= k) if k > 0 else (row < seq + k)
        return jnp.where(keep, pltpu.roll(x, k % seq, axis=0), 0.0)

    for g, window in enumerate(POOL_WINDOWS):
        half = window // 2
        lanes = slice(POOL_GROUP * g, POOL_GROUP * (g + 1))
        u = u_ref[:, lanes]
        ahead = behind = u
        n = 1
        while n < half:
            ahead = ahead + shifted(ahead, -n)
            behind = behind + shifted(behind, n)
            n *= 2
        win_sum = ahead + shifted(behind, 1)
        count = (jnp.minimum(row + half, seq) - jnp.maximum(row - half, 0)).astype(F32)
        d = win_sum / count - u
        o_ref[:, lanes] = (_dot(d.astype(BF16), w_ref[g]) * scale_ref[:, lanes]).astype(BF16)


def _fft_across_blocks(blocks):
    n = len(blocks)
    if n == 1:
        return blocks
    even = _fft_across_blocks(blocks[0::2])
    odd = _fft_across_blocks(blocks[1::2])
    out = [None] * n
    for k in range(n // 2):
        o_re, o_im = odd[k]
        if k == 0:
            t_re, t_im = o_re, o_im
        elif 4 * k == n:
            t_re, t_im = o_im, -o_re
        else:
            c, s = math.cos(2 * math.pi * k / n), -math.sin(2 * math.pi * k / n)
            t_re, t_im = o_re * c - o_im * s, o_re * s + o_im * c
        e_re, e_im = even[k]
        out[k] = (e_re + t_re, e_im + t_im)
        out[k + n // 2] = (e_re - t_re, e_im - t_im)
    return out


def _fourier_kernel(*refs, radix):
    if radix > 1:
        x_ref, block_ref, chan_ref, tw_cos_ref, tw_sin_ref, o_ref, x32, w_scr = refs
        for g in range(FOURIER_GROUPS):
            x32[g] = x_ref[:, FOURIER_GROUP * g:FOURIER_GROUP * (g + 1)].astype(F32)
    else:
        x_ref, block_ref, chan_ref, o_ref, w_scr = refs
    seq = x_ref.shape[0]
    m = seq // radix

    blocks = []
    for j in range(radix):
        if radix > 1:
            x_j = jnp.concatenate([x32[g, pl.ds(j, m, stride=radix), :] for g in range(FOURIER_GROUPS)],
                                  axis=1).astype(BF16)
        else:
            x_j = x_ref[...]
        g = _dot(block_ref[...], x_j)
        g_re, g_im = g[0:m], g[m:2 * m]
        if j:
            c = jnp.concatenate([tw_cos_ref[j]] * FOURIER_GROUPS, axis=1)
            s = jnp.concatenate([tw_sin_ref[j]] * FOURIER_GROUPS, axis=1)
            g_re, g_im = g_re * c + g_im * s, g_im * c - g_re * s
        blocks.append((g_re, g_im))

    for k1, (w_re, w_im) in enumerate(_fft_across_blocks(blocks)):
        rows = slice(m * k1, m * (k1 + 1))
        w_scr[rows, 0:FOURIER_WIDTH] = w_re.astype(BF16)
        w_scr[rows, FOURIER_WIDTH:] = w_im.astype(BF16)

    scale = 1.0 / math.sqrt(seq * FOURIER_GROUP)
    for g in range(FOURIER_GROUPS):
        lanes = slice(FOURIER_GROUP * g, FOURIER_GROUP * (g + 1))
        im_lanes = slice(FOURIER_WIDTH + FOURIER_GROUP * g, FOURIER_WIDTH + FOURIER_GROUP * (g + 1))
        w = jnp.concatenate([w_scr[:, lanes], w_scr[:, im_lanes]], axis=1)
        o_ref[:, lanes] = (_dot(w, chan_ref[...]) * scale).astype(BF16)


def _sequence_mix_kernel(u_ref, w_pool_ref, scale_ref, *refs, radix):
    n_fourier_in = 5 if radix > 1 else 3
    pool_out_ref = refs[n_fourier_in]
    _pool_kernel(u_ref, w_pool_ref, scale_ref, pool_out_ref)
    _fourier_kernel(*refs[:n_fourier_in], *refs[n_fourier_in + 1:], radix=radix)


def _sequence_mix(u, x, p, layer, tables):
    batch, seq, _ = x.shape
    radix = seq // DFT_BLOCK
    assert radix * DFT_BLOCK == seq and radix & (radix - 1) == 0
    block, chan, tw_cos, tw_sin = tables
    whole = lambda t: pl.BlockSpec(t.shape, lambda b: (0,) * t.ndim)
    per_seq = lambda width: pl.BlockSpec((None, seq, width), lambda b: (b, 0, 0))
    in_specs = [per_seq(POOL_WIDTH), _per_layer(p["w_pool"], layer), _per_layer(p["pool_scale"], layer),
                per_seq(FOURIER_WIDTH), whole(block), whole(chan)]
    args = [u, p["w_pool"], p["pool_scale"], x, block, chan]
    scratch = [pltpu.VMEM((seq, 2 * FOURIER_WIDTH), BF16)]
    if radix > 1:
        in_specs += [whole(tw_cos), whole(tw_sin)]
        args += [tw_cos, tw_sin]
        scratch = [pltpu.VMEM((FOURIER_GROUPS, seq, FOURIER_GROUP), F32)] + scratch
    return _call(
        functools.partial(_sequence_mix_kernel, radix=radix),
        name="sequence_mix",
        grid=(batch,),
        in_specs=in_specs,
        args=args,
        out_specs=[per_seq(POOL_WIDTH), per_seq(FOURIER_WIDTH)],
        out_shape=[jax.ShapeDtypeStruct((batch, seq, POOL_WIDTH), BF16),
                   jax.ShapeDtypeStruct((batch, seq, FOURIER_WIDTH), BF16)],
        scratch=scratch,
        live_bytes=2 * 2 * seq * FOURIER_WIDTH * 4,
    )


def _mixer_out_kernel(x_ref, mod_ref, h_ref, a_ref, p_ref, f_ref,
                      wg_ref, wpa_ref, wpp_ref, wpf_ref, wo_ref, g_ref, w13_ref, w2_ref,
                      o_ref, h_scr, acc_scr):
    h = h_ref[...]

    def gated(branch_ref, w_ref, index):
        gate = jax.nn.sigmoid(_dot(h, wg_ref[:, D_MODEL * index:D_MODEL * (index + 1)]))
        return gate * _dot(branch_ref[...], w_ref[...])

    m = gated(a_ref, wpa_ref, 0) + gated(p_ref, wpp_ref, 1) + gated(f_ref, wpf_ref, 2)
    x = x_ref[...] + mod_ref[5:6, :] * _dot(m.astype(BF16), wo_ref[...])
    o_ref[...] = _ffn_block(x, mod_ref, g_ref, w13_ref, w2_ref, h_scr, acc_scr, 6)


def _mixer_out_ffn(x, p, layer, mod_row, h, attn, pooled, fourier):
    n = x.shape[0]
    tok = lambda width: pl.BlockSpec((TOKEN_TILE, width), lambda i: (i, 0))
    names = ["w_gate", "w_pa", "w_pp", "w_pf", "w_out"]
    return _call(
        _mixer_out_kernel,
        name="mixer_out_ffn",
        grid=(n // TOKEN_TILE,),
        in_specs=[tok(D_MODEL), _mod_spec(layer, mod_row(TOKEN_TILE)),
                  tok(D_MODEL), tok(ATTN_WIDTH), tok(POOL_WIDTH), tok(FOURIER_WIDTH)]
                 + [_per_layer(p[name], layer, resident=True) for name in names]
                 + [_per_layer(p["norm_g"], layer, 2),
                    _per_layer(p["w13"], layer, 1, resident=True),
                    _per_layer(p["w2"], layer, 1, resident=True)],
        args=(x, p["mod"], h, attn, pooled, fourier, *[p[name] for name in names],
              p["norm_g"], p["w13"], p["w2"]),
        out_specs=tok(D_MODEL),
        out_shape=jax.ShapeDtypeStruct((n, D_MODEL), F32),
        scratch=[pltpu.VMEM((TOKEN_TILE, D_MODEL), BF16), pltpu.VMEM((TOKEN_TILE, D_MODEL), F32)],
        live_bytes=max(_ffn_live_bytes(TOKEN_TILE), TOKEN_TILE * D_MODEL * 4 * 3),
    )


def _trunk_layer(x, batch, seq, p, layer, mod_row, cache, prev_state, rope_tables, dft_tables):
    x = _ffn(x, p, layer, mod_row)

    outs = _mixer_in(x, p, layer, mod_row, rope_tables, seq=seq, prev_state=prev_state)
    h, q, k, v, up, uf = outs[:6]
    per_seq = lambda t: t.reshape(batch, seq, t.shape[-1])
    cache_k, cache_v = (None, None) if cache is None else cache
    attn = _attention(per_seq(q), per_seq(k), per_seq(v), cache_k, cache_v, p, layer)
    pooled, fourier = _sequence_mix(per_seq(up), per_seq(uf), p, layer, dft_tables)
    flat = lambda t: t.reshape(batch * seq, t.shape[-1])
    x = _mixer_out_ffn(x, p, layer, mod_row, h, flat(attn), flat(pooled), flat(fourier))
    return x, tuple(outs[6:])


def kernel(x_prompt, x_sample, cache_k, cache_v, c, c_ctx, w_ada, b_ada, norm_g, ffn_w13, ffn_w2, w_in, q_norm_g, k_norm_g, lam_qk, subln_g, w_pool, pool_scale, w_gate, w_pa, w_pp, w_pf, w_out):
    batch, seq, _ = x_prompt.shape
    dec_batch, dec_seq, _ = x_sample.shape
    depth = w_ada.shape[0]
    n_past = cache_k.shape[2]
    assert dec_batch + 1 <= MOD_ROWS
    for tile in (TOKEN_TILE, FFN_TOKEN_TILE):
        assert dec_seq % tile == 0 and (batch * seq) % tile == 0

    cond = jnp.concatenate(
        [c, c_ctx[None], jnp.zeros((MOD_ROWS - dec_batch - 1, D_MODEL), F32)], axis=0)

    cast = lambda t: t.astype(BF16)
    rows = lambda t: t.reshape(t.shape[:-1] + (1, t.shape[-1]))
    tile_gain = lambda g: rows(jnp.tile(g, (1, QK_WIDTH // HEAD_DIM)))
    p = {
        "mod": _modulation(cond, w_ada, b_ada),
        "norm_g": rows(norm_g), "w13": cast(ffn_w13), "w2": cast(ffn_w2), "w_in": cast(w_in),
        "seg": cast(jnp.asarray(_segment_mean_matrix())),
        "q_gain": tile_gain(q_norm_g), "k_gain": tile_gain(k_norm_g),
        "lam_qk": lam_qk, "subln_g": rows(subln_g),
        "w_pool": cast(w_pool), "pool_scale": rows(pool_scale),
        "w_gate": cast(w_gate), "w_pa": cast(w_pa), "w_pp": cast(w_pp), "w_pf": cast(w_pf),
        "w_out": cast(w_out),
    }
    def dft_tables(n):
        block, chan, tw_cos, tw_sin = (jnp.asarray(t) for t in _dft_tables(n))
        return cast(block), cast(chan), tw_cos, tw_sin
    dft_ctx, dft_lat = dft_tables(seq), dft_tables(dec_seq)
    rope_lat = tuple(jnp.asarray(t) for t in _rope_tables(dec_seq))
    cache = (cache_k.reshape(dec_batch, depth, n_past * N_HEADS, 2 * HEAD_DIM),
             cache_v.reshape(dec_batch, depth, n_past * N_HEADS, V_DIM))

    ctx_row = lambda tile: (lambda i: dec_batch)
    lat_row = lambda tile: (lambda i: (i * tile) // dec_seq)

    xp = x_prompt.reshape(batch * seq, D_MODEL)
    xs = x_sample.reshape(dec_batch * dec_seq, D_MODEL)
    state = ()
    for l in range(depth):
        xp, state = _trunk_layer(xp, batch, seq, p, l, ctx_row, None, state, None, dft_ctx)
        xs, _ = _trunk_layer(xs, dec_batch, dec_seq, p, l, lat_row, cache, None, rope_lat, dft_lat)

    state_k, state_v = state
    return (xp.reshape(batch, seq, D_MODEL), xs.reshape(dec_batch, dec_seq, D_MODEL),
            state_k.reshape(batch, depth, seq, N_HEADS, 2 * HEAD_DIM),
            state_v.reshape(batch, depth, seq, N_HEADS, V_DIM))
```

```python
import functools
import math

import numpy as np
import jax
import jax.numpy as jnp
from jax import lax
from jax.experimental import pallas as pl
from jax.experimental.pallas import tpu as pltpu

D_MODEL = 1024
N_HEADS = 8
HEAD_DIM = 64
V_DIM = 2 * HEAD_DIM
QK_WIDTH = N_HEADS * 2 * HEAD_DIM
ATTN_WIDTH = N_HEADS * V_DIM
POOL_WINDOWS = (2, 4, 8, 16)
POOL_GROUP = 128
POOL_WIDTH = POOL_GROUP * len(POOL_WINDOWS)
FOURIER_GROUPS = 4
FOURIER_GROUP = 128
FOURIER_WIDTH = FOURIER_GROUPS * FOURIER_GROUP
D_FF = 2816
N_MOD = 9
GRID_W = 64
ROPE_BASE = 10000.0
EPS = 1e-6

LANES = 128
MXU_WIDTH = 256
V7X_VMEM_BYTES = 64 * 1024 * 1024
PIPELINE_BUFFERS = 2

MOD_ROWS = 16
MOD_COL_TILE = 2304
FF_CHUNK = 256
TOKEN_TILE = 512
FFN_TOKEN_TILE = 1024
Q_TILE = 2048
KEY_CHUNK = 256
ATTN_KEYS_PER_STEP = 4096
LOG2_E = math.log2(math.e)
DFT_BLOCK = MXU_WIDTH

F32 = jnp.float32
BF16 = jnp.bfloat16


def _dot(a, b):
    return jnp.dot(a, b, preferred_element_type=F32)


def _dot_nt(a, b):
    return lax.dot_general(a, b, (((1,), (1,)), ((), ())), preferred_element_type=F32)


def _resident(block_shape, index_map):
    return pl.BlockSpec(block_shape, index_map, pipeline_mode=pl.Buffered(1))


def _per_layer(array, *lead, resident=False):
    tail = array.shape[len(lead):]
    index = tuple(lead) + (0,) * len(tail)
    make = _resident if resident else pl.BlockSpec
    return make((None,) * len(lead) + tail, lambda *_: index)


def _mod_spec(layer, mod_row):
    return pl.BlockSpec((None, None, N_MOD, D_MODEL), lambda i: (layer, mod_row(i), 0, 0))


def _window_bytes(spec, dtype):
    if spec.block_shape is None:
        return 0
    single = spec.pipeline_mode is not None and spec.pipeline_mode.buffer_count == 1
    elements = math.prod(d for d in spec.block_shape if d is not None)
    return elements * jnp.dtype(dtype).itemsize * (1 if single else PIPELINE_BUFFERS)


def _call(body, *, name, grid, in_specs, args, out_specs, out_shape, live_bytes, scratch=(), aliases=None):
    outs, out_list = (out_shape, out_specs) if isinstance(out_shape, (list, tuple)) else ([out_shape], [out_specs])
    planned = (sum(_window_bytes(s, a.dtype) for s, a in zip(in_specs, args))
               + sum(_window_bytes(s, o.dtype) for s, o in zip(out_list, outs))
               + sum(math.prod(s.shape) * jnp.dtype(s.dtype).itemsize for s in scratch)
               + live_bytes)
    assert planned <= V7X_VMEM_BYTES, (name, planned)
    return pl.pallas_call(
        body,
        out_shape=out_shape,
        grid=grid,
        in_specs=in_specs,
        out_specs=out_specs,
        scratch_shapes=list(scratch),
        input_output_aliases=aliases or {},
        compiler_params=pltpu.CompilerParams(
            dimension_semantics=("arbitrary",) * len(grid), vmem_limit_bytes=planned),
        name=name,
    )(*args)


def _modulated_norm(x, gain, mod_ref, first):
    y = x * lax.rsqrt(jnp.mean(x * x, axis=-1, keepdims=True) + EPS) * gain
    return y * (1.0 + mod_ref[first + 1:first + 2, :]) + mod_ref[first:first + 1, :]


@functools.lru_cache(maxsize=None)
def _rope_tables(seq):
    n_freq = HEAD_DIM // 4
    lane = np.arange(LANES)
    within = lane % HEAD_DIM
    axis = within // (2 * n_freq)
    half = (within % (2 * n_freq)) // n_freq
    freq = lane % n_freq
    inv = ROPE_BASE ** (-np.arange(n_freq, dtype=np.float64) / n_freq)
    pos = np.arange(seq)
    coord = np.where(axis[None, :] == 0, (pos // GRID_W)[:, None], (pos % GRID_W)[:, None])
    ang = coord.astype(np.float64) * inv[freq][None, :]
    cos, sin = np.cos(ang), np.sin(ang)
    sin_first = np.where(half[None, :] == 0, -sin, 0.0)
    sin_second = np.where(half[None, :] == 1, sin, 0.0)
    return tuple(np.asarray(t, np.float32) for t in (cos, sin_first, sin_second))


@functools.lru_cache(maxsize=None)
def _segment_mean_matrix():
    idx = np.arange(MXU_WIDTH) // HEAD_DIM
    return np.asarray((idx[:, None] == idx[None, :]) / HEAD_DIM, np.float32)


@functools.lru_cache(maxsize=None)
def _dft_tables(seq):
    def cs(n, rows, cols):
        ang = 2.0 * np.pi * ((rows[:, None] * cols[None, :]) % n).astype(np.float64) / n
        return np.cos(ang), np.sin(ang)
    k_block = np.arange(DFT_BLOCK, dtype=np.int64)
    k_chan = np.arange(FOURIER_GROUP, dtype=np.int64)
    c_block, s_block = cs(DFT_BLOCK, k_block, k_block)
    c_chan, s_chan = cs(FOURIER_GROUP, k_chan, k_chan)
    c_tw, s_tw = cs(seq, np.arange(seq // DFT_BLOCK, dtype=np.int64), k_block)
    lanes = lambda t: np.repeat(t[:, :, None], LANES, axis=2)
    tables = (np.concatenate([c_block, -s_block], axis=0), np.concatenate([c_chan, s_chan], axis=0),
              lanes(c_tw), lanes(s_tw))
    return tuple(np.asarray(t, np.float32) for t in tables)


def _mod_kernel(c_ref, w_ref, b_ref, o_ref):
    c = c_ref[...]
    s = c * jax.nn.sigmoid(c)
    s_hi = s.astype(BF16)
    s_lo = (s - s_hi.astype(F32)).astype(BF16)
    w = w_ref[...]
    w_hi = w.astype(BF16)
    w_lo = (w - w_hi.astype(F32)).astype(BF16)
    o_ref[...] = _dot(s_hi, w_hi) + _dot(s_lo, w_hi) + _dot(s_hi, w_lo) + b_ref[...]


def _modulation(cond, w_ada, b_ada):
    depth = w_ada.shape[0]
    width = N_MOD * D_MODEL
    out = _call(
        _mod_kernel,
        name="adaln_modulation",
        grid=(depth, width // MOD_COL_TILE),
        in_specs=[
            pl.BlockSpec((MOD_ROWS, D_MODEL), lambda l, n: (0, 0)),
            pl.BlockSpec((None, D_MODEL, MOD_COL_TILE), lambda l, n: (l, 0, n)),
            pl.BlockSpec((None, 1, MOD_COL_TILE), lambda l, n: (l, 0, n)),
        ],
        args=(cond, w_ada, b_ada.reshape(depth, 1, width)),
        out_specs=pl.BlockSpec((None, MOD_ROWS, MOD_COL_TILE), lambda l, n: (l, 0, n)),
        out_shape=jax.ShapeDtypeStruct((depth, MOD_ROWS, width), F32),
        live_bytes=D_MODEL * MOD_COL_TILE * (4 + 4 + 2 + 2),
    )
    return out.reshape(depth, MOD_ROWS, N_MOD, D_MODEL)


def _ffn_block(x, mod_ref, g_ref, w13_ref, w2_ref, h_scr, acc_scr, mod_first):
    h_scr[...] = _modulated_norm(x, g_ref[...], mod_ref, mod_first).astype(BF16)
    acc_scr[...] = jnp.zeros_like(acc_scr)

    for j in range(D_FF // FF_CHUNK):
        cols = slice(FF_CHUNK * j, FF_CHUNK * (j + 1))
        h = h_scr[...]
        gate = _dot(h, w13_ref[:, cols])
        up = _dot(h, w13_ref[:, D_FF + FF_CHUNK * j:D_FF + FF_CHUNK * (j + 1)])
        act = (gate * jax.nn.sigmoid(gate)) * up
        acc_scr[...] += _dot(act.astype(BF16), w2_ref[cols, :])

    return x + (0.5 * mod_ref[mod_first + 2:mod_first + 3, :]) * acc_scr[...]


def _ffn_kernel(x_ref, mod_ref, g_ref, w13_ref, w2_ref, o_ref, h_scr, acc_scr):
    o_ref[...] = _ffn_block(x_ref[...], mod_ref, g_ref, w13_ref, w2_ref, h_scr, acc_scr, 0)


def _ffn_live_bytes(tile):
    return tile * (2 * D_MODEL * 4 + FF_CHUNK * (3 * 4 + 2))


def _ffn(x, p, layer, mod_row):
    n = x.shape[0]
    half = 0
    tile = FFN_TOKEN_TILE
    return _call(
        _ffn_kernel,
        name="swiglu_ffn",
        grid=(n // tile,),
        in_specs=[
            pl.BlockSpec((tile, D_MODEL), lambda i: (i, 0)),
            _mod_spec(layer, mod_row(tile)),
            _per_layer(p["norm_g"], layer, 2 * half),
            _per_layer(p["w13"], layer, half, resident=True),
            _per_layer(p["w2"], layer, half, resident=True),
        ],
        args=(x, p["mod"], p["norm_g"], p["w13"], p["w2"]),
        out_specs=pl.BlockSpec((tile, D_MODEL), lambda i: (i, 0)),
        out_shape=jax.ShapeDtypeStruct((n, D_MODEL), F32),
        scratch=[pltpu.VMEM((tile, D_MODEL), BF16), pltpu.VMEM((tile, D_MODEL), F32)],
        live_bytes=_ffn_live_bytes(tile),
    )


def _mixer_in_kernel(*refs, rope, state, with_ffn):
    refs = list(refs)
    x_ref, mod_ref, g_ref, win_ref, seg_ref, gq_ref, gk_ref = refs[:7]
    refs = refs[7:]
    if rope:
        cos_ref, sin_first_ref, sin_second_ref = refs[:3]
        refs = refs[3:]
    if with_ffn:
        ffn_gain_ref, w13_ref, w2_ref = refs[:3]
        h_scr, acc_scr = refs[-2:]
        refs = refs[3:-2]
    if state == "update":
        refs = refs[2:]
    if with_ffn:
        x_out_ref = refs[0]
        refs = refs[1:]
    h_ref, q_ref, k_ref, v_ref, up_ref, uf_ref = refs[:6]
    refs = refs[6:]
    if state:
        sk_ref, sv_ref = refs

    def store_state(ref, t):
        t = t.reshape(ref.shape[0], ref.shape[-2], ref.shape[-1])
        if state == "first":
            for d in range(ref.shape[1]):
                ref[:, d] = t
        else:
            ref[...] = t

    x = x_ref[...]
    if with_ffn:
        x = _ffn_block(x, mod_ref, ffn_gain_ref, w13_ref, w2_ref, h_scr, acc_scr, 0)
        x_out_ref[...] = x
    h = _modulated_norm(x, g_ref[...], mod_ref, 3).astype(BF16)
    h_ref[...] = h

    def head_norm(t, gain):
        sq = (t * t).astype(BF16)
        width = seg_ref.shape[0]
        mean_sq = jnp.concatenate(
            [_dot(sq[:, width * j:width * (j + 1)], seg_ref[...]) for j in range(QK_WIDTH // width)],
            axis=1)
        return t * lax.rsqrt(mean_sq + EPS) * gain

    def head_chunks(t):
        return [t[:, LANES * i:LANES * (i + 1)] for i in range(N_HEADS)]

    def rotate(chunk):
        return (chunk * cos_ref[...]
                + pltpu.roll(chunk, LANES - HEAD_DIM // 4, axis=1) * sin_first_ref[...]
                + pltpu.roll(chunk, HEAD_DIM // 4, axis=1) * sin_second_ref[...])

    q = head_norm(_dot(h, win_ref[:, 0:QK_WIDTH]), gq_ref[...])
    for i, chunk in enumerate(head_chunks(q)):
        if rope:
            chunk = rotate(chunk)
        q_ref[:, LANES * i:LANES * (i + 1)] = (chunk * (HEAD_DIM ** -0.5 * LOG2_E)).astype(BF16)

    k = head_norm(_dot(h, win_ref[:, QK_WIDTH:2 * QK_WIDTH]), gk_ref[...])
    if state:
        store_state(sk_ref, k)
    for i, chunk in enumerate(head_chunks(k)):
        if rope:
            chunk = rotate(chunk)
        k_ref[:, LANES * i:LANES * (i + 1)] = chunk.astype(BF16)

    v0 = 2 * QK_WIDTH
    v = _dot(h, win_ref[:, v0:v0 + ATTN_WIDTH])
    if state:
        store_state(sv_ref, v)
    v_ref[...] = v.astype(BF16)

    p0 = v0 + ATTN_WIDTH
    up_ref[...] = _dot(h, win_ref[:, p0:p0 + POOL_WIDTH])
    f0 = p0 + POOL_WIDTH
    uf_ref[...] = _dot(h, win_ref[:, f0:f0 + FOURIER_WIDTH]).astype(BF16)


def _mixer_in(x, p, layer, mod_row, rope_tables, *, seq, prev_state, with_ffn):
    n = x.shape[0]
    rope = rope_tables is not None
    tiles_per_seq = seq // TOKEN_TILE
    tok = lambda width: pl.BlockSpec((TOKEN_TILE, width), lambda i: (i, 0))
    in_specs = [
        tok(D_MODEL),
        _mod_spec(layer, mod_row(TOKEN_TILE)),
        _per_layer(p["norm_g"], layer, 1),
        _per_layer(p["w_in"], layer, resident=True),
        _resident(p["seg"].shape, lambda i: (0, 0)),
        _per_layer(p["q_gain"], layer),
        _per_layer(p["k_gain"], layer),
    ]
    args = [x, p["mod"], p["norm_g"], p["w_in"], p["seg"], p["q_gain"], p["k_gain"]]
    if rope:
        in_specs += [pl.BlockSpec((TOKEN_TILE, LANES), lambda i: (i % tiles_per_seq, 0))] * 3
        args += list(rope_tables)
    scratch, live_bytes = [], 0
    if with_ffn:
        in_specs += [_per_layer(p["norm_g"], layer, 0),
                     _per_layer(p["w13"], layer, 0, resident=True),
                     _per_layer(p["w2"], layer, 0, resident=True)]
        args += [p["norm_g"], p["w13"], p["w2"]]
        scratch = [pltpu.VMEM((TOKEN_TILE, D_MODEL), BF16), pltpu.VMEM((TOKEN_TILE, D_MODEL), F32)]
        live_bytes = _ffn_live_bytes(TOKEN_TILE)
    out_shape = [jax.ShapeDtypeStruct((n, D_MODEL), BF16),
                 jax.ShapeDtypeStruct((n, QK_WIDTH), BF16),
                 jax.ShapeDtypeStruct((n, QK_WIDTH), BF16),
                 jax.ShapeDtypeStruct((n, ATTN_WIDTH), BF16),
                 jax.ShapeDtypeStruct((n, POOL_WIDTH), F32),
                 jax.ShapeDtypeStruct((n, FOURIER_WIDTH), BF16)]
    out_specs = [tok(D_MODEL), tok(QK_WIDTH), tok(QK_WIDTH), tok(ATTN_WIDTH),
                 tok(POOL_WIDTH), tok(FOURIER_WIDTH)]
    if with_ffn:
        out_shape = [jax.ShapeDtypeStruct((n, D_MODEL), F32)] + out_shape
        out_specs = [tok(D_MODEL)] + out_specs
    state, aliases = None, {}
    if prev_state is not None:
        depth = p["w_in"].shape[0]
        per_tile = TOKEN_TILE // seq
        assert per_tile * seq == TOKEN_TILE
        out_shape += [jax.ShapeDtypeStruct((n // seq, depth, seq, width), F32)
                      for width in (QK_WIDTH, ATTN_WIDTH)]
        if prev_state:
            state = "update"
            slot = lambda width: pl.BlockSpec((per_tile, None, seq, width), lambda i: (i, layer, 0, 0))
            aliases = {len(args): len(out_specs), len(args) + 1: len(out_specs) + 1}
            in_specs += [pl.BlockSpec(memory_space=pl.ANY)] * 2
            args += list(prev_state)
        else:
            state = "first"
            slot = lambda width: pl.BlockSpec((per_tile, depth, seq, width), lambda i: (i, 0, 0, 0))
        out_specs += [slot(QK_WIDTH), slot(ATTN_WIDTH)]
    return _call(
        functools.partial(_mixer_in_kernel, rope=rope, state=state, with_ffn=with_ffn),
        name="ffn_mixer_in" if with_ffn else "mixer_in",
        grid=(n // TOKEN_TILE,),
        in_specs=in_specs,
        args=args,
        out_specs=out_specs,
        out_shape=out_shape,
        scratch=scratch,
        aliases=aliases,
        live_bytes=max(live_bytes, TOKEN_TILE * D_MODEL * (4 + 2 + 4 + 2 + 4 + 4)),
    )


def _attention_kernel(*refs, n_past, lam_init, heads):
    refs = list(refs)
    q_ref, k_ref, v_ref = refs[:3]
    refs = refs[3:]
    if n_past:
        ck_ref, cv_ref = refs[:2]
        refs = refs[2:]
    lam_ref, gain_ref, o_ref = refs[:3]
    head_lanes = lambda j: slice(LANES * j, LANES * (j + 1))

    n_q = q_ref.shape[0]
    n_keys = n_past + k_ref.shape[0]
    key_chunk = min(KEY_CHUNK, k_ref.shape[0])
    past_chunks = n_past // key_chunk

    if n_past:
        k_past, v_past = refs[3:]
        group = pl.program_id(1)

        @pl.when((pl.program_id(2) == 0) & (group == 0))
        def _():
            for h in range(N_HEADS):
                k_past[h] = ck_ref[pl.ds(h, n_past, stride=N_HEADS), :].astype(BF16)
                v_past[h] = cv_ref[pl.ds(h, n_past, stride=N_HEADS), :].astype(BF16)

    def chunk_of(past, new, j, c):
        if c < past_chunks:
            return past[group * heads + j, key_chunk * c:key_chunk * (c + 1), :]
        c -= past_chunks
        return new[key_chunk * c:key_chunk * (c + 1), head_lanes(j)]

    keys_of = lambda j, c: chunk_of(k_past if n_past else None, k_ref, j, c)
    values_of = lambda j, c: chunk_of(v_past if n_past else None, v_ref, j, c)

    lq = lam_ref[...]
    lam = (jnp.exp(jnp.sum(lq[0:1, :] * lq[1:2, :], axis=-1, keepdims=True))
           - jnp.exp(jnp.sum(lq[2:3, :] * lq[3:4, :], axis=-1, keepdims=True)) + lam_init)

    lane = lax.broadcasted_iota(jnp.int32, (n_q, LANES), 1)
    zero = jnp.zeros((n_q, LANES), BF16)

    for j in range(heads):
        q = q_ref[:, head_lanes(j)]
        q_both = jnp.concatenate(
            [jnp.where(lane < HEAD_DIM, q, zero), jnp.where(lane >= HEAD_DIM, q, zero)], axis=0)

        run_max = acc = den = None
        for c in range(n_keys // key_chunk):
            s = _dot_nt(q_both, keys_of(j, c))
            new_max = jnp.max(s, axis=-1, keepdims=True)
            if c:
                new_max = jnp.maximum(run_max, new_max)
            e = jnp.exp2(s - new_max)
            pv = _dot(e.astype(BF16), values_of(j, c))
            part = e[:, 0:LANES]
            for t in range(1, key_chunk // LANES):
                part = part + e[:, LANES * t:LANES * (t + 1)]
            if c == 0:
                acc, den = pv, part
            else:
                alpha = jnp.exp2(run_max - new_max)
                acc, den = acc * alpha + pv, den * alpha + part
            run_max = new_max
        o_both = acc / jnp.sum(den, axis=-1, keepdims=True)

        o = o_both[0:n_q] - lam * o_both[n_q:]
        o = o * lax.rsqrt(jnp.mean(o * o, axis=-1, keepdims=True) + EPS) * gain_ref[...]
        o_ref[:, head_lanes(j)] = (o * (1.0 - lam_init)).astype(BF16)


def _attention(q, k, v, cache_k, cache_v, p, layer):
    batch, seq, _ = q.shape
    lam_init = 0.8 - 0.6 * math.exp(-0.3 * layer)
    n_past = 0 if cache_k is None else cache_k.shape[2] // N_HEADS
    n_keys = n_past + seq
    key_chunk = min(KEY_CHUNK, seq)
    assert seq % key_chunk == 0 and n_past % key_chunk == 0
    q_tile = min(Q_TILE, seq)
    heads = max(1, min(N_HEADS, ATTN_KEYS_PER_STEP // n_keys))
    width = LANES * heads
    seq_block = pl.BlockSpec((None, seq, width), lambda b, g, i: (b, 0, g))
    in_specs = [pl.BlockSpec((None, q_tile, width), lambda b, g, i: (b, i, g)), seq_block, seq_block]
    args = [q, k, v]
    scratch = []
    if n_past:
        past_block = pl.BlockSpec((None, None, n_past * N_HEADS, LANES), lambda b, g, i: (b, layer, 0, 0))
        in_specs += [past_block, past_block]
        args += [cache_k, cache_v]
        scratch = [pltpu.VMEM((N_HEADS, n_past, LANES), BF16), pltpu.VMEM((N_HEADS, n_past, V_DIM), BF16)]
    in_specs += [_per_layer(p["lam_qk"], layer), _per_layer(p["subln_g"], layer)]
    args += [p["lam_qk"], p["subln_g"]]
    rows = 2 * q_tile
    return _call(
        functools.partial(_attention_kernel, n_past=n_past, lam_init=lam_init, heads=heads),
        name="diff_attention",
        grid=(batch, N_HEADS // heads, seq // q_tile),
        in_specs=in_specs,
        args=args,
        out_specs=pl.BlockSpec((None, q_tile, width), lambda b, g, i: (b, i, g)),
        out_shape=jax.ShapeDtypeStruct((batch, seq, ATTN_WIDTH), BF16),
        scratch=scratch,
        live_bytes=heads * rows * (2 * key_chunk * (4 + 4 + 2) + 4 * LANES * 4),
    )


def _pool_kernel(u_ref, w_ref, scale_ref, o_ref):
    seq = u_ref.shape[0]
    row = lax.broadcasted_iota(jnp.int32, (seq, POOL_GROUP), 0)

    def shifted(x, k):
        keep = (row >= k) if k > 0 else (row < seq + k)
        return jnp.where(keep, pltpu.roll(x, k % seq, axis=0), 0.0)

    for g, window in enumerate(POOL_WINDOWS):
        half = window // 2
        lanes = slice(POOL_GROUP * g, POOL_GROUP * (g + 1))
        u = u_ref[:, lanes]
        ahead = behind = u
        n = 1
        while n < half:
            ahead = ahead + shifted(ahead, -n)
            behind = behind + shifted(behind, n)
            n *= 2
        win_sum = ahead + shifted(behind, 1)
        count = (jnp.minimum(row + half, seq) - jnp.maximum(row - half, 0)).astype(F32)
        d = win_sum / count - u
        o_ref[:, lanes] = (_dot(d.astype(BF16), w_ref[g]) * scale_ref[:, lanes]).astype(BF16)


def _fft_across_blocks(blocks):
    n = len(blocks)
    if n == 1:
        return blocks
    even = _fft_across_blocks(blocks[0::2])
    odd = _fft_across_blocks(blocks[1::2])
    out = [None] * n
    for k in range(n // 2):
        o_re, o_im = odd[k]
        if k == 0:
            t_re, t_im = o_re, o_im
        elif 4 * k == n:
            t_re, t_im = o_im, -o_re
        else:
            c, s = math.cos(2 * math.pi * k / n), -math.sin(2 * math.pi * k / n)
            t_re, t_im = o_re * c - o_im * s, o_re * s + o_im * c
        e_re, e_im = even[k]
        out[k] = (e_re + t_re, e_im + t_im)
        out[k + n // 2] = (e_re - t_re, e_im - t_im)
    return out


def _fourier_kernel(*refs, radix):
    if radix > 1:
        x_ref, block_ref, chan_ref, tw_cos_ref, tw_sin_ref, o_ref, x32, w_scr = refs
        for g in range(FOURIER_GROUPS):
            x32[g] = x_ref[:, FOURIER_GROUP * g:FOURIER_GROUP * (g + 1)].astype(F32)
    else:
        x_ref, block_ref, chan_ref, o_ref, w_scr = refs
    seq = x_ref.shape[0]
    m = seq // radix

    blocks = []
    for j in range(radix):
        if radix > 1:
            x_j = jnp.concatenate([x32[g, pl.ds(j, m, stride=radix), :] for g in range(FOURIER_GROUPS)],
                                  axis=1).astype(BF16)
        else:
            x_j = x_ref[...]
        g = _dot(block_ref[...], x_j)
        g_re, g_im = g[0:m], g[m:2 * m]
        if j:
            c = jnp.concatenate([tw_cos_ref[j]] * FOURIER_GROUPS, axis=1)
            s = jnp.concatenate([tw_sin_ref[j]] * FOURIER_GROUPS, axis=1)
            g_re, g_im = g_re * c + g_im * s, g_im * c - g_re * s
        blocks.append((g_re, g_im))

    for k1, (w_re, w_im) in enumerate(_fft_across_blocks(blocks)):
        rows = slice(m * k1, m * (k1 + 1))
        w_scr[rows, 0:FOURIER_WIDTH] = w_re.astype(BF16)
        w_scr[rows, FOURIER_WIDTH:] = w_im.astype(BF16)

    scale = 1.0 / math.sqrt(seq * FOURIER_GROUP)
    for g in range(FOURIER_GROUPS):
        lanes = slice(FOURIER_GROUP * g, FOURIER_GROUP * (g + 1))
        im_lanes = slice(FOURIER_WIDTH + FOURIER_GROUP * g, FOURIER_WIDTH + FOURIER_GROUP * (g + 1))
        w = jnp.concatenate([w_scr[:, lanes], w_scr[:, im_lanes]], axis=1)
        o_ref[:, lanes] = (_dot(w, chan_ref[...]) * scale).astype(BF16)


def _sequence_mix_kernel(u_ref, w_pool_ref, scale_ref, *refs, radix):
    n_fourier_in = 5 if radix > 1 else 3
    pool_out_ref = refs[n_fourier_in]
    _pool_kernel(u_ref, w_pool_ref, scale_ref, pool_out_ref)
    _fourier_kernel(*refs[:n_fourier_in], *refs[n_fourier_in + 1:], radix=radix)


def _sequence_mix(u, x, p, layer, tables):
    batch, seq, _ = x.shape
    radix = seq // DFT_BLOCK
    assert radix * DFT_BLOCK == seq and radix & (radix - 1) == 0
    block, chan, tw_cos, tw_sin = tables
    whole = lambda t: pl.BlockSpec(t.shape, lambda b: (0,) * t.ndim)
    per_seq = lambda width: pl.BlockSpec((None, seq, width), lambda b: (b, 0, 0))
    in_specs = [per_seq(POOL_WIDTH), _per_layer(p["w_pool"], layer), _per_layer(p["pool_scale"], layer),
                per_seq(FOURIER_WIDTH), whole(block), whole(chan)]
    args = [u, p["w_pool"], p["pool_scale"], x, block, chan]
    scratch = [pltpu.VMEM((seq, 2 * FOURIER_WIDTH), BF16)]
    if radix > 1:
        in_specs += [whole(tw_cos), whole(tw_sin)]
        args += [tw_cos, tw_sin]
        scratch = [pltpu.VMEM((FOURIER_GROUPS, seq, FOURIER_GROUP), F32)] + scratch
    return _call(
        functools.partial(_sequence_mix_kernel, radix=radix),
        name="sequence_mix",
        grid=(batch,),
        in_specs=in_specs,
        args=args,
        out_specs=[per_seq(POOL_WIDTH), per_seq(FOURIER_WIDTH)],
        out_shape=[jax.ShapeDtypeStruct((batch, seq, POOL_WIDTH), BF16),
                   jax.ShapeDtypeStruct((batch, seq, FOURIER_WIDTH), BF16)],
        scratch=scratch,
        live_bytes=2 * 2 * seq * FOURIER_WIDTH * 4,
    )


def _mixer_out_kernel(x_ref, mod_ref, h_ref, a_ref, p_ref, f_ref,
                      wg_ref, wpa_ref, wpp_ref, wpf_ref, wo_ref, g_ref, w13_ref, w2_ref,
                      o_ref, h_scr, acc_scr):
    h = h_ref[...]

    def gated(branch_ref, w_ref, index):
        gate = jax.nn.sigmoid(_dot(h, wg_ref[:, D_MODEL * index:D_MODEL * (index + 1)]))
        return gate * _dot(branch_ref[...], w_ref[...])

    m = gated(a_ref, wpa_ref, 0) + gated(p_ref, wpp_ref, 1) + gated(f_ref, wpf_ref, 2)
    x = x_ref[...] + mod_ref[5:6, :] * _dot(m.astype(BF16), wo_ref[...])
    o_ref[...] = _ffn_block(x, mod_ref, g_ref, w13_ref, w2_ref, h_scr, acc_scr, 6)


def _mixer_out_ffn(x, p, layer, mod_row, h, attn, pooled, fourier):
    n = x.shape[0]
    tok = lambda width: pl.BlockSpec((TOKEN_TILE, width), lambda i: (i, 0))
    names = ["w_gate", "w_pa", "w_pp", "w_pf", "w_out"]
    return _call(
        _mixer_out_kernel,
        name="mixer_out_ffn",
        grid=(n // TOKEN_TILE,),
        in_specs=[tok(D_MODEL), _mod_spec(layer, mod_row(TOKEN_TILE)),
                  tok(D_MODEL), tok(ATTN_WIDTH), tok(POOL_WIDTH), tok(FOURIER_WIDTH)]
                 + [_per_layer(p[name], layer, resident=True) for name in names]
                 + [_per_layer(p["norm_g"], layer, 2),
                    _per_layer(p["w13"], layer, 1, resident=True),
                    _per_layer(p["w2"], layer, 1, resident=True)],
        args=(x, p["mod"], h, attn, pooled, fourier, *[p[name] for name in names],
              p["norm_g"], p["w13"], p["w2"]),
        out_specs=tok(D_MODEL),
        out_shape=jax.ShapeDtypeStruct((n, D_MODEL), F32),
        scratch=[pltpu.VMEM((TOKEN_TILE, D_MODEL), BF16), pltpu.VMEM((TOKEN_TILE, D_MODEL), F32)],
        live_bytes=max(_ffn_live_bytes(TOKEN_TILE), TOKEN_TILE * D_MODEL * 4 * 3),
    )


def _trunk_layer(x, batch, seq, p, layer, mod_row, cache, prev_state, rope_tables, dft_tables):
    with_ffn = prev_state is None
    if not with_ffn:
        x = _ffn(x, p, layer, mod_row)
    outs = _mixer_in(x, p, layer, mod_row, rope_tables, seq=seq, prev_state=prev_state, with_ffn=with_ffn)
    if with_ffn:
        x, outs = outs[0], outs[1:]
    h, q, k, v, up, uf = outs[:6]
    per_seq = lambda t: t.reshape(batch, seq, t.shape[-1])
    cache_k, cache_v = (None, None) if cache is None else cache
    attn = _attention(per_seq(q), per_seq(k), per_seq(v), cache_k, cache_v, p, layer)
    pooled, fourier = _sequence_mix(per_seq(up), per_seq(uf), p, layer, dft_tables)
    flat = lambda t: t.reshape(batch * seq, t.shape[-1])
    x = _mixer_out_ffn(x, p, layer, mod_row, h, flat(attn), flat(pooled), flat(fourier))
    return x, tuple(outs[6:])


def kernel(x_prompt, x_sample, cache_k, cache_v, c, c_ctx, w_ada, b_ada, norm_g, ffn_w13, ffn_w2, w_in, q_norm_g, k_norm_g, lam_qk, subln_g, w_pool, pool_scale, w_gate, w_pa, w_pp, w_pf, w_out):
    batch, seq, _ = x_prompt.shape
    dec_batch, dec_seq, _ = x_sample.shape
    depth = w_ada.shape[0]
    n_past = cache_k.shape[2]
    assert dec_batch + 1 <= MOD_ROWS
    for tile in (TOKEN_TILE, FFN_TOKEN_TILE):
        assert dec_seq % tile == 0 and (batch * seq) % tile == 0

    cond = jnp.concatenate(
        [c, c_ctx[None], jnp.zeros((MOD_ROWS - dec_batch - 1, D_MODEL), F32)], axis=0)

    cast = lambda t: t.astype(BF16)
    rows = lambda t: t.reshape(t.shape[:-1] + (1, t.shape[-1]))
    tile_gain = lambda g: rows(jnp.tile(g, (1, QK_WIDTH // HEAD_DIM)))
    p = {
        "mod": _modulation(cond, w_ada, b_ada),
        "norm_g": rows(norm_g), "w13": cast(ffn_w13), "w2": cast(ffn_w2), "w_in": cast(w_in),
        "seg": cast(jnp.asarray(_segment_mean_matrix())),
        "q_gain": tile_gain(q_norm_g), "k_gain": tile_gain(k_norm_g),
        "lam_qk": lam_qk, "subln_g": rows(subln_g),
        "w_pool": cast(w_pool), "pool_scale": rows(pool_scale),
        "w_gate": cast(w_gate), "w_pa": cast(w_pa), "w_pp": cast(w_pp), "w_pf": cast(w_pf),
        "w_out": cast(w_out),
    }
    def dft_tables(n):
        block, chan, tw_cos, tw_sin = (jnp.asarray(t) for t in _dft_tables(n))
        return cast(block), cast(chan), tw_cos, tw_sin
    dft_ctx, dft_lat = dft_tables(seq), dft_tables(dec_seq)
    rope_lat = tuple(jnp.asarray(t) for t in _rope_tables(dec_seq))
    cache = (cache_k.reshape(dec_batch, depth, n_past * N_HEADS, 2 * HEAD_DIM),
             cache_v.reshape(dec_batch, depth, n_past * N_HEADS, V_DIM))

    ctx_row = lambda tile: (lambda i: dec_batch)
    lat_row = lambda tile: (lambda i: (i * tile) // dec_seq)

    xp = x_prompt.reshape(batch * seq, D_MODEL)
    xs = x_sample.reshape(dec_batch * dec_seq, D_MODEL)
    state = ()
    for l in range(depth):
        xp, state = _trunk_layer(xp, batch, seq, p, l, ctx_row, None, state, None, dft_ctx)
        xs, _ = _trunk_layer(xs, dec_batch, dec_seq, p, l, lat_row, cache, None, rope_lat, dft_lat)

    state_k, state_v = state
    return (xp.reshape(batch, seq, D_MODEL), xs.reshape(dec_batch, dec_seq, D_MODEL),
            state_k.reshape(batch, depth, seq, N_HEADS, 2 * HEAD_DIM),
            state_v.reshape(batch, depth, seq, N_HEADS, V_DIM))
```

```python
import functools
import math

import numpy as np
import jax
import jax.numpy as jnp
from jax import lax
from jax.experimental import pallas as pl
from jax.experimental.pallas import tpu as pltpu

D_MODEL = 1024
N_HEADS = 8
HEAD_DIM = 64
V_DIM = 2 * HEAD_DIM
QK_WIDTH = N_HEADS * 2 * HEAD_DIM
ATTN_WIDTH = N_HEADS * V_DIM
POOL_WINDOWS = (2, 4, 8, 16)
POOL_GROUP = 128
POOL_WIDTH = POOL_GROUP * len(POOL_WINDOWS)
FOURIER_GROUPS = 4
FOURIER_GROUP = 128
FOURIER_WIDTH = FOURIER_GROUPS * FOURIER_GROUP
D_FF = 2816
N_MOD = 9
GRID_W = 64
ROPE_BASE = 10000.0
EPS = 1e-6

LANES = 128
MXU_WIDTH = 256
V7X_VMEM_BYTES = 64 * 1024 * 1024
PIPELINE_BUFFERS = 2

MOD_ROWS = 16
MOD_COL_TILE = 2304
FF_CHUNK = MXU_WIDTH
TOKEN_TILE = 512
FFN_TOKEN_TILE = 1024
Q_TILE = 2048
KEY_CHUNK = MXU_WIDTH
ATTN_KEYS_PER_STEP = 4096
LOG2_E = math.log2(math.e)
DFT_BLOCK = MXU_WIDTH

F32 = jnp.float32
BF16 = jnp.bfloat16


def _dot(a, b):
    return jnp.dot(a, b, preferred_element_type=F32)


def _dot_nt(a, b):
    return lax.dot_general(a, b, (((1,), (1,)), ((), ())), preferred_element_type=F32)


def _resident(block_shape, index_map):
    return pl.BlockSpec(block_shape, index_map, pipeline_mode=pl.Buffered(1))


def _per_layer(array, *lead, resident=False):
    tail = array.shape[len(lead):]
    index = tuple(lead) + (0,) * len(tail)
    make = _resident if resident else pl.BlockSpec
    return make((None,) * len(lead) + tail, lambda *_: index)


def _mod_spec(layer, mod_row):
    return pl.BlockSpec((None, None, N_MOD, D_MODEL), lambda i: (layer, mod_row(i), 0, 0))


def _window_bytes(spec, dtype):
    if spec.block_shape is None:
        return 0
    single = spec.pipeline_mode is not None and spec.pipeline_mode.buffer_count == 1
    elements = math.prod(d for d in spec.block_shape if d is not None)
    return elements * jnp.dtype(dtype).itemsize * (1 if single else PIPELINE_BUFFERS)


def _call(body, *, name, grid, in_specs, args, out_specs, out_shape, live_bytes, scratch=(), aliases=None):
    outs, out_list = (out_shape, out_specs) if isinstance(out_shape, (list, tuple)) else ([out_shape], [out_specs])
    planned = (sum(_window_bytes(s, a.dtype) for s, a in zip(in_specs, args))
               + sum(_window_bytes(s, o.dtype) for s, o in zip(out_list, outs))
               + sum(math.prod(s.shape) * jnp.dtype(s.dtype).itemsize for s in scratch)
               + live_bytes)
    assert planned <= V7X_VMEM_BYTES, (name, planned)
    return pl.pallas_call(
        body,
        out_shape=out_shape,
        grid=grid,
        in_specs=in_specs,
        out_specs=out_specs,
        scratch_shapes=list(scratch),
        input_output_aliases=aliases or {},
        compiler_params=pltpu.CompilerParams(
            dimension_semantics=("arbitrary",) * len(grid), vmem_limit_bytes=planned),
        name=name,
    )(*args)


def _modulated_norm(x, gain, mod_ref, first):
    y = x * lax.rsqrt(jnp.mean(x * x, axis=-1, keepdims=True) + EPS) * gain
    return y * (1.0 + mod_ref[first + 1:first + 2, :]) + mod_ref[first:first + 1, :]


@functools.lru_cache(maxsize=None)
def _rope_tables(seq):
    n_freq = HEAD_DIM // 4
    lane = np.arange(LANES)
    within = lane % HEAD_DIM
    axis = within // (2 * n_freq)
    half = (within % (2 * n_freq)) // n_freq
    freq = lane % n_freq
    inv = ROPE_BASE ** (-np.arange(n_freq, dtype=np.float64) / n_freq)
    pos = np.arange(seq)
    coord = np.where(axis[None, :] == 0, (pos // GRID_W)[:, None], (pos % GRID_W)[:, None])
    ang = coord.astype(np.float64) * inv[freq][None, :]
    cos, sin = np.cos(ang), np.sin(ang)
    sin_first = np.where(half[None, :] == 0, -sin, 0.0)
    sin_second = np.where(half[None, :] == 1, sin, 0.0)
    return tuple(np.asarray(t, np.float32) for t in (cos, sin_first, sin_second))


@functools.lru_cache(maxsize=None)
def _segment_mean_matrix():
    idx = np.arange(MXU_WIDTH) // HEAD_DIM
    return np.asarray((idx[:, None] == idx[None, :]) / HEAD_DIM, np.float32)


@functools.lru_cache(maxsize=None)
def _dft_tables(seq):
    def cs(n, rows, cols):
        ang = 2.0 * np.pi * ((rows[:, None] * cols[None, :]) % n).astype(np.float64) / n
        return np.cos(ang), np.sin(ang)
    k_block = np.arange(DFT_BLOCK, dtype=np.int64)
    k_chan = np.arange(FOURIER_GROUP, dtype=np.int64)
    c_block, s_block = cs(DFT_BLOCK, k_block, k_block)
    c_chan, s_chan = cs(FOURIER_GROUP, k_chan, k_chan)
    c_tw, s_tw = cs(seq, np.arange(seq // DFT_BLOCK, dtype=np.int64), k_block)
    lanes = lambda t: np.repeat(t[:, :, None], LANES, axis=2)
    tables = (np.concatenate([c_block, -s_block], axis=0), np.concatenate([c_chan, s_chan], axis=0),
              lanes(c_tw), lanes(s_tw))
    return tuple(np.asarray(t, np.float32) for t in tables)


def _mod_kernel(c_ref, w_ref, b_ref, o_ref):
    c = c_ref[...]
    s = c * jax.nn.sigmoid(c)
    s_hi = s.astype(BF16)
    s_lo = (s - s_hi.astype(F32)).astype(BF16)
    w = w_ref[...]
    w_hi = w.astype(BF16)
    w_lo = (w - w_hi.astype(F32)).astype(BF16)
    o_ref[...] = _dot(s_hi, w_hi) + _dot(s_lo, w_hi) + _dot(s_hi, w_lo) + b_ref[...]


def _modulation(cond, w_ada, b_ada):
    depth = w_ada.shape[0]
    width = N_MOD * D_MODEL
    out = _call(
        _mod_kernel,
        name="adaln_modulation",
        grid=(depth, width // MOD_COL_TILE),
        in_specs=[
            pl.BlockSpec((MOD_ROWS, D_MODEL), lambda l, n: (0, 0)),
            pl.BlockSpec((None, D_MODEL, MOD_COL_TILE), lambda l, n: (l, 0, n)),
            pl.BlockSpec((None, 1, MOD_COL_TILE), lambda l, n: (l, 0, n)),
        ],
        args=(cond, w_ada, b_ada.reshape(depth, 1, width)),
        out_specs=pl.BlockSpec((None, MOD_ROWS, MOD_COL_TILE), lambda l, n: (l, 0, n)),
        out_shape=jax.ShapeDtypeStruct((depth, MOD_ROWS, width), F32),
        live_bytes=D_MODEL * MOD_COL_TILE * (4 + 4 + 2 + 2),
    )
    return out.reshape(depth, MOD_ROWS, N_MOD, D_MODEL)


def _ffn_block(x, mod_ref, g_ref, w13_ref, w2_ref, h_scr, acc_scr, mod_first):
    h_scr[...] = _modulated_norm(x, g_ref[...], mod_ref, mod_first).astype(BF16)
    acc_scr[...] = jnp.zeros_like(acc_scr)

    for j in range(D_FF // FF_CHUNK):
        cols = slice(FF_CHUNK * j, FF_CHUNK * (j + 1))
        h = h_scr[...]
        gate = _dot(h, w13_ref[:, cols])
        up = _dot(h, w13_ref[:, D_FF + FF_CHUNK * j:D_FF + FF_CHUNK * (j + 1)])
        act = (gate * jax.nn.sigmoid(gate)) * up
        acc_scr[...] += _dot(act.astype(BF16), w2_ref[cols, :])

    return x + (0.5 * mod_ref[mod_first + 2:mod_first + 3, :]) * acc_scr[...]


def _ffn_kernel(x_ref, mod_ref, g_ref, w13_ref, w2_ref, o_ref, h_scr, acc_scr):
    o_ref[...] = _ffn_block(x_ref[...], mod_ref, g_ref, w13_ref, w2_ref, h_scr, acc_scr, 0)


def _ffn_live_bytes(tile):
    return tile * (2 * D_MODEL * 4 + FF_CHUNK * (3 * 4 + 2))


def _ffn(x, p, layer, mod_row):
    n = x.shape[0]
    half = 0
    tile = FFN_TOKEN_TILE
    return _call(
        _ffn_kernel,
        name="swiglu_ffn",
        grid=(n // tile,),
        in_specs=[
            pl.BlockSpec((tile, D_MODEL), lambda i: (i, 0)),
            _mod_spec(layer, mod_row(tile)),
            _per_layer(p["norm_g"], layer, 2 * half),
            _per_layer(p["w13"], layer, half, resident=True),
            _per_layer(p["w2"], layer, half, resident=True),
        ],
        args=(x, p["mod"], p["norm_g"], p["w13"], p["w2"]),
        out_specs=pl.BlockSpec((tile, D_MODEL), lambda i: (i, 0)),
        out_shape=jax.ShapeDtypeStruct((n, D_MODEL), F32),
        scratch=[pltpu.VMEM((tile, D_MODEL), BF16), pltpu.VMEM((tile, D_MODEL), F32)],
        live_bytes=_ffn_live_bytes(tile),
    )


def _mixer_in_kernel(*refs, rope, state, with_ffn):
    refs = list(refs)
    x_ref, mod_ref, g_ref, win_ref, seg_ref, gq_ref, gk_ref = refs[:7]
    refs = refs[7:]
    if rope:
        cos_ref, sin_first_ref, sin_second_ref = refs[:3]
        refs = refs[3:]
    if with_ffn:
        ffn_gain_ref, w13_ref, w2_ref = refs[:3]
        h_scr, acc_scr = refs[-2:]
        refs = refs[3:-2]
    if state == "update":
        refs = refs[2:]
    if with_ffn:
        x_out_ref = refs[0]
        refs = refs[1:]
    h_ref, q_ref, k_ref, v_ref, up_ref, uf_ref = refs[:6]
    refs = refs[6:]
    if state:
        sk_ref, sv_ref = refs

    def store_state(ref, t):
        t = t.reshape(ref.shape[0], ref.shape[-2], ref.shape[-1])
        if state == "first":
            for d in range(ref.shape[1]):
                ref[:, d] = t
        else:
            ref[...] = t

    x = x_ref[...]
    if with_ffn:
        x = _ffn_block(x, mod_ref, ffn_gain_ref, w13_ref, w2_ref, h_scr, acc_scr, 0)
        x_out_ref[...] = x
    h = _modulated_norm(x, g_ref[...], mod_ref, 3).astype(BF16)
    h_ref[...] = h

    def head_norm(t, gain):
        sq = (t * t).astype(BF16)
        width = seg_ref.shape[0]
        mean_sq = jnp.concatenate(
            [_dot(sq[:, width * j:width * (j + 1)], seg_ref[...]) for j in range(QK_WIDTH // width)],
            axis=1)
        return t * lax.rsqrt(mean_sq + EPS) * gain

    def head_chunks(t):
        return [t[:, LANES * i:LANES * (i + 1)] for i in range(N_HEADS)]

    def rotate(chunk):
        return (chunk * cos_ref[...]
                + pltpu.roll(chunk, LANES - HEAD_DIM // 4, axis=1) * sin_first_ref[...]
                + pltpu.roll(chunk, HEAD_DIM // 4, axis=1) * sin_second_ref[...])

    q = head_norm(_dot(h, win_ref[:, 0:QK_WIDTH]), gq_ref[...])
    for i, chunk in enumerate(head_chunks(q)):
        if rope:
            chunk = rotate(chunk)
        q_ref[:, LANES * i:LANES * (i + 1)] = (chunk * (HEAD_DIM ** -0.5 * LOG2_E)).astype(BF16)

    k = head_norm(_dot(h, win_ref[:, QK_WIDTH:2 * QK_WIDTH]), gk_ref[...])
    if state:
        store_state(sk_ref, k)
    for i, chunk in enumerate(head_chunks(k)):
        if rope:
            chunk = rotate(chunk)
        k_ref[:, LANES * i:LANES * (i + 1)] = chunk.astype(BF16)

    v0 = 2 * QK_WIDTH
    v = _dot(h, win_ref[:, v0:v0 + ATTN_WIDTH])
    if state:
        store_state(sv_ref, v)
    v_ref[...] = v.astype(BF16)

    p0 = v0 + ATTN_WIDTH
    up_ref[...] = _dot(h, win_ref[:, p0:p0 + POOL_WIDTH])
    f0 = p0 + POOL_WIDTH
    uf_ref[...] = _dot(h, win_ref[:, f0:f0 + FOURIER_WIDTH]).astype(BF16)


def _mixer_in(x, p, layer, mod_row, rope_tables, *, seq, prev_state, with_ffn):
    n = x.shape[0]
    rope = rope_tables is not None
    tiles_per_seq = seq // TOKEN_TILE
    tok = lambda width: pl.BlockSpec((TOKEN_TILE, width), lambda i: (i, 0))
    in_specs = [
        tok(D_MODEL),
        _mod_spec(layer, mod_row(TOKEN_TILE)),
        _per_layer(p["norm_g"], layer, 1),
        _per_layer(p["w_in"], layer, resident=True),
        _resident(p["seg"].shape, lambda i: (0, 0)),
        _per_layer(p["q_gain"], layer),
        _per_layer(p["k_gain"], layer),
    ]
    args = [x, p["mod"], p["norm_g"], p["w_in"], p["seg"], p["q_gain"], p["k_gain"]]
    if rope:
        in_specs += [pl.BlockSpec((TOKEN_TILE, LANES), lambda i: (i % tiles_per_seq, 0))] * 3
        args += list(rope_tables)
    scratch, live_bytes = [], 0
    if with_ffn:
        in_specs += [_per_layer(p["norm_g"], layer, 0),
                     _per_layer(p["w13"], layer, 0, resident=True),
                     _per_layer(p["w2"], layer, 0, resident=True)]
        args += [p["norm_g"], p["w13"], p["w2"]]
        scratch = [pltpu.VMEM((TOKEN_TILE, D_MODEL), BF16), pltpu.VMEM((TOKEN_TILE, D_MODEL), F32)]
        live_bytes = _ffn_live_bytes(TOKEN_TILE)
    out_shape = [jax.ShapeDtypeStruct((n, D_MODEL), BF16),
                 jax.ShapeDtypeStruct((n, QK_WIDTH), BF16),
                 jax.ShapeDtypeStruct((n, QK_WIDTH), BF16),
                 jax.ShapeDtypeStruct((n, ATTN_WIDTH), BF16),
                 jax.ShapeDtypeStruct((n, POOL_WIDTH), F32),
                 jax.ShapeDtypeStruct((n, FOURIER_WIDTH), BF16)]
    out_specs = [tok(D_MODEL), tok(QK_WIDTH), tok(QK_WIDTH), tok(ATTN_WIDTH),
                 tok(POOL_WIDTH), tok(FOURIER_WIDTH)]
    if with_ffn:
        out_shape = [jax.ShapeDtypeStruct((n, D_MODEL), F32)] + out_shape
        out_specs = [tok(D_MODEL)] + out_specs
    state, aliases = None, {}
    if prev_state is not None:
        depth = p["w_in"].shape[0]
        per_tile = TOKEN_TILE // seq
        assert per_tile * seq == TOKEN_TILE
        out_shape += [jax.ShapeDtypeStruct((n // seq, depth, seq, width), F32)
                      for width in (QK_WIDTH, ATTN_WIDTH)]
        if prev_state:
            state = "update"
            slot = lambda width: pl.BlockSpec((per_tile, None, seq, width), lambda i: (i, layer, 0, 0))
            aliases = {len(args): len(out_specs), len(args) + 1: len(out_specs) + 1}
            in_specs += [pl.BlockSpec(memory_space=pl.ANY)] * 2
            args += list(prev_state)
        else:
            state = "first"
            slot = lambda width: pl.BlockSpec((per_tile, depth, seq, width), lambda i: (i, 0, 0, 0))
        out_specs += [slot(QK_WIDTH), slot(ATTN_WIDTH)]
    return _call(
        functools.partial(_mixer_in_kernel, rope=rope, state=state, with_ffn=with_ffn),
        name="ffn_mixer_in" if with_ffn else "mixer_in",
        grid=(n // TOKEN_TILE,),
        in_specs=in_specs,
        args=args,
        out_specs=out_specs,
        out_shape=out_shape,
        scratch=scratch,
        aliases=aliases,
        live_bytes=max(live_bytes, TOKEN_TILE * D_MODEL * (4 + 2 + 4 + 2 + 4 + 4)),
    )


def _attention_kernel(*refs, n_past, lam_init, heads):
    refs = list(refs)
    q_ref, k_ref, v_ref = refs[:3]
    refs = refs[3:]
    if n_past:
        ck_ref, cv_ref = refs[:2]
        refs = refs[2:]
    lam_ref, gain_ref, o_ref = refs[:3]
    head_lanes = lambda j: slice(LANES * j, LANES * (j + 1))

    n_q = q_ref.shape[0]
    n_keys = n_past + k_ref.shape[0]
    key_chunk = min(KEY_CHUNK, k_ref.shape[0])
    past_chunks = n_past // key_chunk

    if n_past:
        k_past, v_past = refs[3:]
        group = pl.program_id(1)

        @pl.when((pl.program_id(2) == 0) & (group == 0))
        def _():
            for h in range(N_HEADS):
                k_past[h] = ck_ref[pl.ds(h, n_past, stride=N_HEADS), :].astype(BF16)
                v_past[h] = cv_ref[pl.ds(h, n_past, stride=N_HEADS), :].astype(BF16)

    def chunk_of(past, new, j, c):
        if c < past_chunks:
            return past[group * heads + j, key_chunk * c:key_chunk * (c + 1), :]
        c -= past_chunks
        return new[key_chunk * c:key_chunk * (c + 1), head_lanes(j)]

    keys_of = lambda j, c: chunk_of(k_past if n_past else None, k_ref, j, c)
    values_of = lambda j, c: chunk_of(v_past if n_past else None, v_ref, j, c)

    lq = lam_ref[...]
    lam = (jnp.exp(jnp.sum(lq[0:1, :] * lq[1:2, :], axis=-1, keepdims=True))
           - jnp.exp(jnp.sum(lq[2:3, :] * lq[3:4, :], axis=-1, keepdims=True)) + lam_init)

    lane = lax.broadcasted_iota(jnp.int32, (n_q, LANES), 1)
    zero = jnp.zeros((n_q, LANES), BF16)

    for j in range(heads):
        q = q_ref[:, head_lanes(j)]
        q_both = jnp.concatenate(
            [jnp.where(lane < HEAD_DIM, q, zero), jnp.where(lane >= HEAD_DIM, q, zero)], axis=0)

        run_max = acc = den = None
        for c in range(n_keys // key_chunk):
            s = _dot_nt(q_both, keys_of(j, c))
            new_max = jnp.max(s, axis=-1, keepdims=True)
            if c:
                new_max = jnp.maximum(run_max, new_max)
            e = jnp.exp2(s - new_max)
            pv = _dot(e.astype(BF16), values_of(j, c))
            part = e[:, 0:LANES]
            for t in range(1, key_chunk // LANES):
                part = part + e[:, LANES * t:LANES * (t + 1)]
            if c == 0:
                acc, den = pv, part
            else:
                alpha = jnp.exp2(run_max - new_max)
                acc, den = acc * alpha + pv, den * alpha + part
            run_max = new_max
        o_both = acc / jnp.sum(den, axis=-1, keepdims=True)

        o = o_both[0:n_q] - lam * o_both[n_q:]
        o = o * lax.rsqrt(jnp.mean(o * o, axis=-1, keepdims=True) + EPS) * gain_ref[...]
        o_ref[:, head_lanes(j)] = (o * (1.0 - lam_init)).astype(BF16)


def _attention(q, k, v, cache_k, cache_v, p, layer):
    batch, seq, _ = q.shape
    lam_init = 0.8 - 0.6 * math.exp(-0.3 * layer)
    n_past = 0 if cache_k is None else cache_k.shape[2] // N_HEADS
    n_keys = n_past + seq
    key_chunk = min(KEY_CHUNK, seq)
    assert seq % key_chunk == 0 and n_past % key_chunk == 0
    q_tile = min(Q_TILE, seq)
    heads = max(1, min(N_HEADS, ATTN_KEYS_PER_STEP // n_keys))
    width = LANES * heads
    seq_block = pl.BlockSpec((None, seq, width), lambda b, g, i: (b, 0, g))
    in_specs = [pl.BlockSpec((None, q_tile, width), lambda b, g, i: (b, i, g)), seq_block, seq_block]
    args = [q, k, v]
    scratch = []
    if n_past:
        past_block = pl.BlockSpec((None, None, n_past * N_HEADS, LANES), lambda b, g, i: (b, layer, 0, 0))
        in_specs += [past_block, past_block]
        args += [cache_k, cache_v]
        scratch = [pltpu.VMEM((N_HEADS, n_past, LANES), BF16), pltpu.VMEM((N_HEADS, n_past, V_DIM), BF16)]
    in_specs += [_per_layer(p["lam_qk"], layer), _per_layer(p["subln_g"], layer)]
    args += [p["lam_qk"], p["subln_g"]]
    rows = 2 * q_tile
    return _call(
        functools.partial(_attention_kernel, n_past=n_past, lam_init=lam_init, heads=heads),
        name="diff_attention",
        grid=(batch, N_HEADS // heads, seq // q_tile),
        in_specs=in_specs,
        args=args,
        out_specs=pl.BlockSpec((None, q_tile, width), lambda b, g, i: (b, i, g)),
        out_shape=jax.ShapeDtypeStruct((batch, seq, ATTN_WIDTH), BF16),
        scratch=scratch,
        live_bytes=heads * rows * (2 * key_chunk * (4 + 4 + 2) + 4 * LANES * 4),
    )


def _pool_kernel(u_ref, w_ref, scale_ref, o_ref):
    seq = u_ref.shape[0]
    row = lax.broadcasted_iota(jnp.int32, (seq, POOL_GROUP), 0)

    def shifted(x, k):
        keep = (row >= k) if k > 0 else (row < seq + k)
        return jnp.where(keep, pltpu.roll(x, k % seq, axis=0), 0.0)

    for g, window in enumerate(POOL_WINDOWS):
        half = window // 2
        lanes = slice(POOL_GROUP * g, POOL_GROUP * (g + 1))
        u = u_ref[:, lanes]
        ahead = behind = u
        n = 1
        while n < half:
            ahead = ahead + shifted(ahead, -n)
            behind = behind + shifted(behind, n)
            n *= 2
        win_sum = ahead + shifted(behind, 1)
        count = (jnp.minimum(row + half, seq) - jnp.maximum(row - half, 0)).astype(F32)
        d = win_sum / count - u
        o_ref[:, lanes] = (_dot(d.astype(BF16), w_ref[g]) * scale_ref[:, lanes]).astype(BF16)


def _fft_across_blocks(blocks):
    n = len(blocks)
    if n == 1:
        return blocks
    even = _fft_across_blocks(blocks[0::2])
    odd = _fft_across_blocks(blocks[1::2])
    out = [None] * n
    for k in range(n // 2):
        o_re, o_im = odd[k]
        if k == 0:
            t_re, t_im = o_re, o_im
        elif 4 * k == n:
            t_re, t_im = o_im, -o_re
        else:
            c, s = math.cos(2 * math.pi * k / n), -math.sin(2 * math.pi * k / n)
            t_re, t_im = o_re * c - o_im * s, o_re * s + o_im * c
        e_re, e_im = even[k]
        out[k] = (e_re + t_re, e_im + t_im)
        out[k + n // 2] = (e_re - t_re, e_im - t_im)
    return out


def _fourier_kernel(*refs, radix):
    if radix > 1:
        x_ref, block_ref, chan_ref, tw_cos_ref, tw_sin_ref, o_ref, x32, w_scr = refs
        for g in range(FOURIER_GROUPS):
            x32[g] = x_ref[:, FOURIER_GROUP * g:FOURIER_GROUP * (g + 1)].astype(F32)
    else:
        x_ref, block_ref, chan_ref, o_ref, w_scr = refs
    seq = x_ref.shape[0]
    m = seq // radix

    blocks = []
    for j in range(radix):
        if radix > 1:
            x_j = jnp.concatenate([x32[g, pl.ds(j, m, stride=radix), :] for g in range(FOURIER_GROUPS)],
                                  axis=1).astype(BF16)
        else:
            x_j = x_ref[...]
        g = _dot(block_ref[...], x_j)
        g_re, g_im = g[0:m], g[m:2 * m]
        if j:
            c = jnp.concatenate([tw_cos_ref[j]] * FOURIER_GROUPS, axis=1)
            s = jnp.concatenate([tw_sin_ref[j]] * FOURIER_GROUPS, axis=1)
            g_re, g_im = g_re * c + g_im * s, g_im * c - g_re * s
        blocks.append((g_re, g_im))

    for k1, (w_re, w_im) in enumerate(_fft_across_blocks(blocks)):
        rows = slice(m * k1, m * (k1 + 1))
        w_scr[rows, 0:FOURIER_WIDTH] = w_re.astype(BF16)
        w_scr[rows, FOURIER_WIDTH:] = w_im.astype(BF16)

    scale = 1.0 / math.sqrt(seq * FOURIER_GROUP)
    for g in range(FOURIER_GROUPS):
        lanes = slice(FOURIER_GROUP * g, FOURIER_GROUP * (g + 1))
        im_lanes = slice(FOURIER_WIDTH + FOURIER_GROUP * g, FOURIER_WIDTH + FOURIER_GROUP * (g + 1))
        w = jnp.concatenate([w_scr[:, lanes], w_scr[:, im_lanes]], axis=1)
        o_ref[:, lanes] = (_dot(w, chan_ref[...]) * scale).astype(BF16)


def _sequence_mix_kernel(u_ref, w_pool_ref, scale_ref, *refs, radix):
    n_fourier_in = 5 if radix > 1 else 3
    pool_out_ref = refs[n_fourier_in]
    _pool_kernel(u_ref, w_pool_ref, scale_ref, pool_out_ref)
    _fourier_kernel(*refs[:n_fourier_in], *refs[n_fourier_in + 1:], radix=radix)


def _sequence_mix(u, x, p, layer, tables):
    batch, seq, _ = x.shape
    radix = seq // DFT_BLOCK
    assert radix * DFT_BLOCK == seq and radix & (radix - 1) == 0
    block, chan, tw_cos, tw_sin = tables
    whole = lambda t: pl.BlockSpec(t.shape, lambda b: (0,) * t.ndim)
    per_seq = lambda width: pl.BlockSpec((None, seq, width), lambda b: (b, 0, 0))
    in_specs = [per_seq(POOL_WIDTH), _per_layer(p["w_pool"], layer), _per_layer(p["pool_scale"], layer),
                per_seq(FOURIER_WIDTH), whole(block), whole(chan)]
    args = [u, p["w_pool"], p["pool_scale"], x, block, chan]
    scratch = [pltpu.VMEM((seq, 2 * FOURIER_WIDTH), BF16)]
    if radix > 1:
        in_specs += [whole(tw_cos), whole(tw_sin)]
        args += [tw_cos, tw_sin]
        scratch = [pltpu.VMEM((FOURIER_GROUPS, seq, FOURIER_GROUP), F32)] + scratch
    return _call(
        functools.partial(_sequence_mix_kernel, radix=radix),
        name="sequence_mix",
        grid=(batch,),
        in_specs=in_specs,
        args=args,
        out_specs=[per_seq(POOL_WIDTH), per_seq(FOURIER_WIDTH)],
        out_shape=[jax.ShapeDtypeStruct((batch, seq, POOL_WIDTH), BF16),
                   jax.ShapeDtypeStruct((batch, seq, FOURIER_WIDTH), BF16)],
        scratch=scratch,
        live_bytes=2 * 2 * seq * FOURIER_WIDTH * 4,
    )


def _mixer_out_kernel(x_ref, mod_ref, h_ref, a_ref, p_ref, f_ref,
                      wg_ref, wpa_ref, wpp_ref, wpf_ref, wo_ref, g_ref, w13_ref, w2_ref,
                      o_ref, h_scr, acc_scr):
    h = h_ref[...]

    def gated(branch_ref, w_ref, index):
        gate = jax.nn.sigmoid(_dot(h, wg_ref[:, D_MODEL * index:D_MODEL * (index + 1)]))
        return gate * _dot(branch_ref[...], w_ref[...])

    m = gated(a_ref, wpa_ref, 0) + gated(p_ref, wpp_ref, 1) + gated(f_ref, wpf_ref, 2)
    x = x_ref[...] + mod_ref[5:6, :] * _dot(m.astype(BF16), wo_ref[...])
    o_ref[...] = _ffn_block(x, mod_ref, g_ref, w13_ref, w2_ref, h_scr, acc_scr, 6)


def _mixer_out_ffn(x, p, layer, mod_row, h, attn, pooled, fourier):
    n = x.shape[0]
    tok = lambda width: pl.BlockSpec((TOKEN_TILE, width), lambda i: (i, 0))
    names = ["w_gate", "w_pa", "w_pp", "w_pf", "w_out"]
    return _call(
        _mixer_out_kernel,
        name="mixer_out_ffn",
        grid=(n // TOKEN_TILE,),
        in_specs=[tok(D_MODEL), _mod_spec(layer, mod_row(TOKEN_TILE)),
                  tok(D_MODEL), tok(ATTN_WIDTH), tok(POOL_WIDTH), tok(FOURIER_WIDTH)]
                 + [_per_layer(p[name], layer, resident=True) for name in names]
                 + [_per_layer(p["norm_g"], layer, 2),
                    _per_layer(p["w13"], layer, 1, resident=True),
                    _per_layer(p["w2"], layer, 1, resident=True)],
        args=(x, p["mod"], h, attn, pooled, fourier, *[p[name] for name in names],
              p["norm_g"], p["w13"], p["w2"]),
        out_specs=tok(D_MODEL),
        out_shape=jax.ShapeDtypeStruct((n, D_MODEL), F32),
        scratch=[pltpu.VMEM((TOKEN_TILE, D_MODEL), BF16), pltpu.VMEM((TOKEN_TILE, D_MODEL), F32)],
        live_bytes=max(_ffn_live_bytes(TOKEN_TILE), TOKEN_TILE * D_MODEL * 4 * 3),
    )


def _trunk_layer(x, batch, seq, p, layer, mod_row, cache, prev_state, rope_tables, dft_tables):
    with_ffn = prev_state is None
    if not with_ffn:
        x = _ffn(x, p, layer, mod_row)
    outs = _mixer_in(x, p, layer, mod_row, rope_tables, seq=seq, prev_state=prev_state, with_ffn=with_ffn)
    if with_ffn:
        x, outs = outs[0], outs[1:]
    h, q, k, v, up, uf = outs[:6]
    per_seq = lambda t: t.reshape(batch, seq, t.shape[-1])
    cache_k, cache_v = (None, None) if cache is None else cache
    attn = _attention(per_seq(q), per_seq(k), per_seq(v), cache_k, cache_v, p, layer)
    pooled, fourier = _sequence_mix(per_seq(up), per_seq(uf), p, layer, dft_tables)
    flat = lambda t: t.reshape(batch * seq, t.shape[-1])
    x = _mixer_out_ffn(x, p, layer, mod_row, h, flat(attn), flat(pooled), flat(fourier))
    return x, tuple(outs[6:])


def kernel(x_prompt, x_sample, cache_k, cache_v, c, c_ctx, w_ada, b_ada, norm_g, ffn_w13, ffn_w2, w_in, q_norm_g, k_norm_g, lam_qk, subln_g, w_pool, pool_scale, w_gate, w_pa, w_pp, w_pf, w_out):
    batch, seq, _ = x_prompt.shape
    dec_batch, dec_seq, _ = x_sample.shape
    depth = w_ada.shape[0]
    n_past = cache_k.shape[2]
    assert dec_batch + 1 <= MOD_ROWS
    for tile in (TOKEN_TILE, FFN_TOKEN_TILE):
        assert dec_seq % tile == 0 and (batch * seq) % tile == 0

    cond = jnp.concatenate(
        [c, c_ctx[None], jnp.zeros((MOD_ROWS - dec_batch - 1, D_MODEL), F32)], axis=0)

    cast = lambda t: t.astype(BF16)
    rows = lambda t: t.reshape(t.shape[:-1] + (1, t.shape[-1]))
    tile_gain = lambda g: rows(jnp.tile(g, (1, QK_WIDTH // HEAD_DIM)))
    p = {
        "mod": _modulation(cond, w_ada, b_ada),
        "norm_g": rows(norm_g), "w13": cast(ffn_w13), "w2": cast(ffn_w2), "w_in": cast(w_in),
        "seg": cast(jnp.asarray(_segment_mean_matrix())),
        "q_gain": tile_gain(q_norm_g), "k_gain": tile_gain(k_norm_g),
        "lam_qk": lam_qk, "subln_g": rows(subln_g),
        "w_pool": cast(w_pool), "pool_scale": rows(pool_scale),
        "w_gate": cast(w_gate), "w_pa": cast(w_pa), "w_pp": cast(w_pp), "w_pf": cast(w_pf),
        "w_out": cast(w_out),
    }
    def dft_tables(n):
        block, chan, tw_cos, tw_sin = (jnp.asarray(t) for t in _dft_tables(n))
        return cast(block), cast(chan), tw_cos, tw_sin
    dft_ctx, dft_lat = dft_tables(seq), dft_tables(dec_seq)
    rope_lat = tuple(jnp.asarray(t) for t in _rope_tables(dec_seq))
    cache = (cache_k.reshape(dec_batch, depth, n_past * N_HEADS, 2 * HEAD_DIM),
             cache_v.reshape(dec_batch, depth, n_past * N_HEADS, V_DIM))

    ctx_row = lambda tile: (lambda i: dec_batch)
    lat_row = lambda tile: (lambda i: (i * tile) // dec_seq)

    xp = x_prompt.reshape(batch * seq, D_MODEL)
    xs = x_sample.reshape(dec_batch * dec_seq, D_MODEL)
    state = ()
    for l in range(depth):
        xp, state = _trunk_layer(xp, batch, seq, p, l, ctx_row, None, state, None, dft_ctx)
        xs, _ = _trunk_layer(xs, dec_batch, dec_seq, p, l, lat_row, cache, None, rope_lat, dft_lat)

    state_k, state_v = state
    return (xp.reshape(batch, seq, D_MODEL), xs.reshape(dec_batch, dec_seq, D_MODEL),
            state_k.reshape(batch, depth, seq, N_HEADS, 2 * HEAD_DIM),
            state_v.reshape(batch, depth, seq, N_HEADS, V_DIM))
```
